```python
import math
import jax, jax.numpy as jnp
from jax import lax
import numpy as np

D_MODEL = 1024
BATCH = 8
SEQ = 8192
DEPTH = 2

CTX_LEN = 256
GRID_W = 64
D_MIX = D_MODEL
N_GROUPS = 4
GROUP_W = D_MIX // N_GROUPS
D_FF = 2816
N_MOD = 9
EPS = 1e-6
ROPE_THETA = 10000.0
Q_BLOCK = 128
DA_HEADS = 4
DA_QK = GROUP_W // (2 * DA_HEADS)
DA_V = GROUP_W // DA_HEADS
POOL_WINDOWS = (2, 4, 8, 16)
POOL_GROUP = GROUP_W // len(POOL_WINDOWS)
ML_HEADS = 4
ML_DIM = GROUP_W // ML_HEADS
ML_CHUNK = 64
GQA_HEADS = 4
GQA_KV_HEADS = 2
GQA_DIM = GROUP_W // GQA_HEADS
IN_SPLITS = (GROUP_W, GROUP_W, GROUP_W,
             GROUP_W,
             GROUP_W, GROUP_W, GROUP_W, GROUP_W, 4 * ML_HEADS,
             GROUP_W, GQA_KV_HEADS * GQA_DIM, GQA_KV_HEADS * GQA_DIM)
IN_WIDTH = sum(IN_SPLITS)

kernel_name = 'hybrid_parallel_group_dit_block'


def layer_norm(x, g=None, b=None):
    xf = x.astype(jnp.float32)
    mu = xf.mean(-1, keepdims=True)
    var = jnp.square(xf - mu).mean(-1, keepdims=True)
    y = (xf - mu) * lax.rsqrt(var + EPS)
    if g is not None:
        y = y * g.astype(jnp.float32) + b.astype(jnp.float32)
    return y.astype(x.dtype)


def rms_norm(x, g):
    xf = x.astype(jnp.float32)
    y = xf * lax.rsqrt(jnp.mean(jnp.square(xf), -1, keepdims=True) + EPS) * g.astype(jnp.float32)
    return y.astype(x.dtype)


def rope_tables(row, col, dim):
    axis_dim = dim // 2
    inv = ROPE_THETA ** (-jnp.arange(0, axis_dim, 2, dtype=jnp.float32) / axis_dim)
    ang = jnp.concatenate([row.astype(jnp.float32)[:, None] * inv,
                           col.astype(jnp.float32)[:, None] * inv], axis=-1)
    return jnp.cos(ang), jnp.sin(ang)


def apply_rope(x, cos, sin):
    half = x.shape[-1] // 2
    shape = (1, cos.shape[0]) + (1,) * (x.ndim - 3) + (half,)
    cos = cos.reshape(shape).astype(x.dtype)
    sin = sin.reshape(shape).astype(x.dtype)
    x1, x2 = x[..., :half], x[..., half:]
    return jnp.concatenate([x1 * cos - x2 * sin, x2 * cos + x1 * sin], axis=-1)


def sweep_query_blocks(fn, q):
    B, L = q.shape[0], q.shape[1]
    nb = L // Q_BLOCK
    qb = jnp.moveaxis(q.reshape((B, nb, Q_BLOCK) + q.shape[2:]), 1, 0)
    out = jnp.moveaxis(lax.map(fn, qb), 0, 1)
    return out.reshape((B, L) + out.shape[3:])


def diff_attention_core(q, k, v, lam):
    s = jnp.einsum('bqhmd,bkhmd->bhmqk', q, k).astype(jnp.float32) * DA_QK ** -0.5
    p = jax.nn.softmax(s, axis=-1)
    a = p[:, :, 0] - lam * p[:, :, 1]
    return jnp.einsum('bhqk,bkhe->bqhe', a.astype(v.dtype), v)


def gqa_core(q, k, v):
    s = jnp.einsum('bqhgd,bkhd->bhgqk', q, k).astype(jnp.float32) * GQA_DIM ** -0.5
    p = jax.nn.softmax(s, axis=-1)
    return jnp.einsum('bhgqk,bkhd->bqhgd', p.astype(v.dtype), v)


def multiscale_pool(u, pool_w, pool_scale):
    B, L, _ = u.shape
    uf = u.astype(jnp.float32)
    csum = jnp.concatenate([jnp.zeros((B, 1, GROUP_W), jnp.float32), jnp.cumsum(uf, axis=1)], axis=1)
    t = jnp.arange(L)
    outs = []
    for g, w in enumerate(POOL_WINDOWS):
        lo = jnp.clip(t - w // 2, 0, L)
        hi = jnp.clip(t + w // 2, 0, L)
        sl = slice(g * POOL_GROUP, (g + 1) * POOL_GROUP)
        mean = (csum[:, hi, sl] - csum[:, lo, sl]) / (hi - lo).astype(jnp.float32)[None, :, None]
        outs.append(mean - uf[:, :, sl])
    d = jnp.stack(outs, axis=2)
    y = jnp.einsum('blgc,gce->blge', d, pool_w.astype(jnp.float32)).reshape(B, L, GROUP_W)
    return (y * pool_scale.astype(jnp.float32)).astype(u.dtype)


def mlstm_gates(cg, gate_b):
    B, T, _ = cg.shape
    g = (cg.astype(jnp.float32) + gate_b.astype(jnp.float32).reshape(-1)).reshape(B, T, 4, ML_HEADS)
    g = jnp.transpose(g, (2, 0, 3, 1))
    return (g[0], jax.nn.log_sigmoid(g[1]), g[2], jax.nn.log_sigmoid(g[3]))


def mlstm_zero_state(B):
    return (jnp.zeros((B, ML_HEADS, ML_DIM, ML_DIM), jnp.float32),
            jnp.zeros((B, ML_HEADS, ML_DIM), jnp.float32),
            jnp.zeros((B, ML_HEADS), jnp.float32))


def mlstm_state_update(k, v, li, lf, state):
    C, n, m = state
    b = jnp.cumsum(lf, axis=-1)
    b_end = b[..., -1]
    g = b_end[..., None] - b + li
    m_new = jnp.maximum(b_end + m, g.max(-1))
    wk = jnp.exp(g - m_new[..., None])
    decay = jnp.exp(b_end + m - m_new)
    kf, vf = k.astype(jnp.float32), v.astype(jnp.float32)
    C_new = decay[..., None, None] * C + jnp.einsum('bht,bthv,bthe->bhve', wk, vf, kf)
    n_new = decay[..., None] * n + jnp.einsum('bht,bthe->bhe', wk, kf)
    return (C_new, n_new, m_new)


def mlstm_chunk_output(q, k, v, li, lf, state):
    C, n, m = state
    T = q.shape[1]
    b = jnp.cumsum(lf, axis=-1)
    dmat = b[..., :, None] - b[..., None, :] + li[..., None, :]
    dmat = jnp.where(jnp.tril(jnp.ones((T, T), dtype=bool)), dmat, -jnp.inf)
    inter = b + m[..., None]
    m_t = jnp.maximum(inter, dmat.max(axis=-1))
    w_intra = jnp.exp(dmat - m_t[..., None])
    w_inter = jnp.exp(inter - m_t)
    qf = q.astype(jnp.float32) * ML_DIM ** -0.5
    kf, vf = k.astype(jnp.float32), v.astype(jnp.float32)
    a = jnp.einsum('bjhe,bkhe->bhjk', qf, kf) * w_intra
    num = jnp.einsum('bhjk,bkhv->bjhv', a, vf) + jnp.einsum('bhj,bhve,bjhe->bjhv', w_inter, C, qf)
    den = a.sum(-1) + w_inter * jnp.einsum('bhe,bjhe->bhj', n, qf)
    den = jnp.maximum(jnp.abs(den), jnp.exp(-m_t))
    return num / jnp.swapaxes(den, 1, 2)[..., None]


def mlstm_scan(q, k, v, li, lf, state):
    B, T, H, d = q.shape
    nc = T // ML_CHUNK
    def tok_chunks(a):
        return jnp.moveaxis(a.reshape((B, nc, ML_CHUNK) + a.shape[2:]), 1, 0)
    def gate_chunks(a):
        return jnp.moveaxis(a.reshape(B, H, nc, ML_CHUNK), 2, 0)
    def body(st, xs):
        qc, kc, vc, lic, lfc = xs
        h = mlstm_chunk_output(qc, kc, vc, lic, lfc, st)
        return mlstm_state_update(kc, vc, lic, lfc, st), h
    st, hs = lax.scan(body, state, (tok_chunks(q), tok_chunks(k), tok_chunks(v), gate_chunks(li), gate_chunks(lf)))
    h = jnp.moveaxis(hs, 0, 1).reshape(B, T, H, d)
    return h.astype(q.dtype), st


def mlstm_bidirectional(q, k, v, gates, st_f, st_b):
    li_f, lf_f, li_b, lf_b = gates
    hf, end_f = mlstm_scan(q, k, v, li_f, lf_f, st_f)
    hb, end_b = mlstm_scan(q[:, ::-1], k[:, ::-1], v[:, ::-1], li_b[..., ::-1], lf_b[..., ::-1], st_b)
    return hf + hb[:, ::-1], end_f, end_b


def token_mixing(h_lat, h_ctx, layer, rope_a, rope_d, w_in, w_out, diff_lambda, diff_norm_g,
                 pool_w, pool_scale, ml_gate_b, ml_norm_g, q_norm_g, k_norm_g, ctx_out):
    B, L, _ = h_lat.shape
    Lc = h_ctx.shape[1]
    G = GQA_HEADS // GQA_KV_HEADS
    offsets = np.cumsum(IN_SPLITS)[:-1].tolist()
    aq, ak, av, bu, cq, ck, cv, co, cg, dq, dk, dv = jnp.split(h_lat @ w_in, offsets, axis=-1)
    aq_c, ak_c, av_c, bu_c, cq_c, ck_c, cv_c, co_c, cg_c, dq_c, dk_c, dv_c = jnp.split(h_ctx @ w_in, offsets, axis=-1)

    lam_init = 0.8 - 0.6 * math.exp(-0.3 * layer)
    dl = diff_lambda.astype(jnp.float32)
    lam = jnp.exp(jnp.sum(dl[0] * dl[1])) - jnp.exp(jnp.sum(dl[2] * dl[3])) + lam_init
    def diff_heads(t, n):
        return t.reshape(B, n, DA_HEADS, 2, DA_QK)
    def diff_finish(o):
        n = o.shape[1]
        return (rms_norm(o, diff_norm_g.reshape(DA_HEADS, DA_V)) * (1.0 - lam_init)).reshape(B, n, GROUP_W)
    k_a_c = diff_heads(ak_c, Lc)
    v_a_c = av_c.reshape(B, Lc, DA_HEADS, DA_V)
    k_a = jnp.concatenate([k_a_c, apply_rope(diff_heads(ak, L), *rope_a)], axis=1)
    v_a = jnp.concatenate([v_a_c, av.reshape(B, L, DA_HEADS, DA_V)], axis=1)
    q_a = apply_rope(diff_heads(aq, L), *rope_a)
    a_lat = diff_finish(sweep_query_blocks(lambda qb: diff_attention_core(qb, k_a, v_a, lam), q_a))

    b_lat = multiscale_pool(bu, pool_w, pool_scale)

    def ml_heads(t, n):
        return t.reshape(B, n, ML_HEADS, ML_DIM)
    def ml_finish(h, o):
        n = h.shape[1]
        return rms_norm(h, ml_norm_g.reshape(ML_HEADS, ML_DIM)).reshape(B, n, GROUP_W) * jax.nn.sigmoid(o)
    zero = mlstm_zero_state(B)
    gates_c = mlstm_gates(cg_c, ml_gate_b)
    q_m_c, k_m_c, v_m_c = ml_heads(cq_c, Lc), ml_heads(ck_c, Lc), ml_heads(cv_c, Lc)
    if ctx_out:
        h_m_c, st_f, st_b = mlstm_bidirectional(q_m_c, k_m_c, v_m_c, gates_c, zero, zero)
    else:
        st_f = mlstm_state_update(k_m_c, v_m_c, gates_c[0], gates_c[1], zero)
        st_b = mlstm_state_update(k_m_c[:, ::-1], v_m_c[:, ::-1], gates_c[2][..., ::-1], gates_c[3][..., ::-1], zero)
    h_m, _, _ = mlstm_bidirectional(ml_heads(cq, L), ml_heads(ck, L), ml_heads(cv, L),
                                    mlstm_gates(cg, ml_gate_b), st_f, st_b)
    c_lat = ml_finish(h_m, co)

    def gq_heads(t, n):
        return rms_norm(t.reshape(B, n, GQA_KV_HEADS, G, GQA_DIM), q_norm_g)
    def gk_heads(t, n):
        return rms_norm(t.reshape(B, n, GQA_KV_HEADS, GQA_DIM), k_norm_g)
    k_d_c = gk_heads(dk_c, Lc)
    v_d_c = dv_c.reshape(B, Lc, GQA_KV_HEADS, GQA_DIM)
    k_d = jnp.concatenate([k_d_c, apply_rope(gk_heads(dk, L), *rope_d)], axis=1)
    v_d = jnp.concatenate([v_d_c, dv.reshape(B, L, GQA_KV_HEADS, GQA_DIM)], axis=1)
    q_d = apply_rope(gq_heads(dq, L), *rope_d)
    d_lat = sweep_query_blocks(lambda qb: gqa_core(qb, k_d, v_d), q_d).reshape(B, L, GROUP_W)

    out_lat = jnp.concatenate([a_lat, b_lat, c_lat, d_lat], axis=-1) @ w_out
    if not ctx_out:
        return out_lat, None
    a_c = diff_finish(diff_attention_core(diff_heads(aq_c, Lc), k_a_c, v_a_c, lam))
    b_c = multiscale_pool(bu_c, pool_w, pool_scale)
    c_c = ml_finish(h_m_c, co_c)
    d_c = gqa_core(gq_heads(dq_c, Lc), k_d_c, v_d_c).reshape(B, Lc, GROUP_W)
    out_ctx = jnp.concatenate([a_c, b_c, c_c, d_c], axis=-1) @ w_out
    return out_lat, out_ctx


def modulate(x, mod, s):
    return layer_norm(x) * (1.0 + mod[:, 3 * s + 1]) + mod[:, 3 * s]


def swiglu(h, wi, wo):
    gate, up = jnp.split(h @ wi, 2, axis=-1)
    return (jax.nn.silu(gate) * up) @ wo


def macaron_ffn(x, mod, s, wi, wo, g, b, alpha):
    h = modulate(x, mod, s)
    return layer_norm(alpha * x + 0.5 * mod[:, 3 * s + 2] * swiglu(h, wi, wo), g, b)


def setup_inputs(seed: int = 0) -> dict:
    key = jax.random.key(seed)
    ks = jax.random.split(key, 22)
    def nrm(k, shape, s):
        return jax.random.normal(k, shape, jnp.float32) * s
    beta = (8.0 * DEPTH) ** -0.25
    gate_offset = jnp.array([0.0, 3.0, 0.0, 3.0], jnp.float32)[None, :, None]
    return {
        'x': nrm(ks[0], (BATCH, SEQ, D_MODEL), 1.0),
        'c': nrm(ks[1], (BATCH, D_MODEL), 1.0),
        'ctx': nrm(ks[2], (BATCH, CTX_LEN, D_MODEL), 1.0),
        'c_ctx': nrm(ks[3], (D_MODEL,), 1.0),
        'w_ada': nrm(ks[4], (DEPTH, D_MODEL, N_MOD * D_MODEL), 0.5 * D_MODEL ** -0.5),
        'b_ada': nrm(ks[5], (DEPTH, N_MOD * D_MODEL), 0.02),
        'ln_g': 1.0 + nrm(ks[6], (DEPTH, 3, D_MODEL), 0.02),
        'ln_b': nrm(ks[7], (DEPTH, 3, D_MODEL), 0.02),
        'ffn1_wi': nrm(ks[8], (DEPTH, D_MODEL, 2 * D_FF), D_MODEL ** -0.5),
        'ffn1_wo': nrm(ks[9], (DEPTH, D_FF, D_MODEL), beta * D_FF ** -0.5),
        'ffn2_wi': nrm(ks[10], (DEPTH, D_MODEL, 2 * D_FF), D_MODEL ** -0.5),
        'ffn2_wo': nrm(ks[11], (DEPTH, D_FF, D_MODEL), beta * D_FF ** -0.5),
        'w_in': nrm(ks[12], (DEPTH, D_MODEL, IN_WIDTH), D_MODEL ** -0.5),
        'w_out': nrm(ks[13], (DEPTH, D_MIX, D_MODEL), beta * D_MIX ** -0.5),
        'diff_lambda': nrm(ks[14], (DEPTH, 4, DA_QK), 0.1),
        'diff_norm_g': 1.0 + nrm(ks[15], (DEPTH, GROUP_W), 0.02),
        'pool_w': nrm(ks[16], (DEPTH, len(POOL_WINDOWS), POOL_GROUP, POOL_GROUP), POOL_GROUP ** -0.5),
        'pool_scale': 1.0 + nrm(ks[17], (DEPTH, GROUP_W), 0.1),
        'ml_gate_b': gate_offset + nrm(ks[18], (DEPTH, 4, ML_HEADS), 0.1),
        'ml_norm_g': 1.0 + nrm(ks[19], (DEPTH, GROUP_W), 0.02),
        'gqa_qnorm_g': 1.0 + nrm(ks[20], (DEPTH, GQA_DIM), 0.02),
        'gqa_knorm_g': 1.0 + nrm(ks[21], (DEPTH, GQA_DIM), 0.02),
    }


def reference(x, c, ctx, c_ctx, w_ada, b_ada, ln_g, ln_b, ffn1_wi, ffn1_wo, ffn2_wi, ffn2_wo,
              w_in, w_out, diff_lambda, diff_norm_g, pool_w, pool_scale, ml_gate_b, ml_norm_g,
              gqa_qnorm_g, gqa_knorm_g):
    B, L, _ = x.shape
    rows = L // GRID_W
    row = jnp.repeat(jnp.arange(rows), GRID_W)
    col = jnp.tile(jnp.arange(GRID_W), rows)
    rope_a = rope_tables(row, col, DA_QK)
    rope_d = rope_tables(row, col, GQA_DIM)
    alpha = (2.0 * DEPTH) ** 0.25
    x_ctx = ctx
    for l in range(DEPTH):
        last = l == DEPTH - 1
        mod_l = (jax.nn.silu(c) @ w_ada[l] + b_ada[l]).reshape(B, N_MOD, 1, D_MODEL)
        mod_c = (jax.nn.silu(c_ctx)[None] @ w_ada[l] + b_ada[l]).reshape(1, N_MOD, 1, D_MODEL)
        x = macaron_ffn(x, mod_l, 0, ffn1_wi[l], ffn1_wo[l], ln_g[l, 0], ln_b[l, 0], alpha)
        x_ctx = macaron_ffn(x_ctx, mod_c, 0, ffn1_wi[l], ffn1_wo[l], ln_g[l, 0], ln_b[l, 0], alpha)
        o_lat, o_ctx = token_mixing(modulate(x, mod_l, 1), modulate(x_ctx, mod_c, 1), l, rope_a, rope_d,
                                    w_in[l], w_out[l], diff_lambda[l], diff_norm_g[l], pool_w[l],
                                    pool_scale[l], ml_gate_b[l], ml_norm_g[l], gqa_qnorm_g[l],
                                    gqa_knorm_g[l], not last)
        x = layer_norm(alpha * x + mod_l[:, 5] * o_lat, ln_g[l, 1], ln_b[l, 1])
        if not last:
            x_ctx = layer_norm(alpha * x_ctx + mod_c[:, 5] * o_ctx, ln_g[l, 1], ln_b[l, 1])
            x_ctx = macaron_ffn(x_ctx, mod_c, 2, ffn2_wi[l], ffn2_wo[l], ln_g[l, 2], ln_b[l, 2], alpha)
        x = macaron_ffn(x, mod_l, 2, ffn2_wi[l], ffn2_wo[l], ln_g[l, 2], ln_b[l, 2], alpha)
    return x
```

```python
import functools
import math

import jax
import jax.numpy as jnp
from jax import lax
from jax.experimental import pallas as pl
from jax.experimental.pallas import tpu as pltpu

F32 = jnp.float32
BF16 = jnp.bfloat16

D_MODEL = 1024
DEPTH = 2
GRID_W = 64
GROUP_W = 256
D_FF = 2816
N_MOD = 9
EPS = 1e-6
ROPE_THETA = 10000.0
DA_HEADS = 4
DA_QK = 32
DA_V = 64
POOL_WINDOWS = (2, 4, 8, 16)
POOL_GROUP = 64
POOL_HALO = 8
ML_HEADS = 4
ML_DIM = 64
GQA_HEADS = 4
GQA_KV_HEADS = 2
GQA_DIM = 64
HEAD_W = 64
LANES = 128
ALPHA = (2.0 * DEPTH) ** 0.25
LOG2E = math.log2(math.e)

TM = 256
FF_CHUNKS = ((0, 1024), (1024, 1024), (2048, 768))
IN_W = 2688
MOD_ROWS = 16
MOD_TN = 1152
VMEM_LIMIT = 56 * 1024 * 1024


def _cparams(sem):
    return pltpu.CompilerParams(dimension_semantics=sem, vmem_limit_bytes=VMEM_LIMIT)


def _const_spec(shape):
    nd = len(shape)
    return pl.BlockSpec(shape, lambda b, j: (0,) * nd, pipeline_mode=pl.Buffered(1))


def _mod_spec():
    return pl.BlockSpec((1, N_MOD, D_MODEL), lambda b, j: (jnp.where(j == 0, 0, b + 1), 0, 0))


def _layer_norm(x):
    mu = jnp.mean(x, axis=-1, keepdims=True)
    xc = x - mu
    var = jnp.mean(xc * xc, axis=-1, keepdims=True)
    return xc * lax.rsqrt(var + EPS)


def _group_mean(x, gm):
    return jnp.dot(x, gm, preferred_element_type=F32, precision=lax.Precision.HIGHEST)


def _mod_kernel(c_ref, w_ref, b_ref, o_ref):
    c = c_ref[...]
    s = c * jax.nn.sigmoid(c)
    o_ref[0] = jnp.dot(s, w_ref[0], preferred_element_type=F32,
                       precision=lax.Precision.HIGHEST) + b_ref[0]


def _modulation(cc, w_ada, b_ada):
    depth = w_ada.shape[0]
    n = N_MOD * D_MODEL
    return pl.pallas_call(
        _mod_kernel,
        grid=(depth, n // MOD_TN),
        in_specs=[pl.BlockSpec((MOD_ROWS, D_MODEL), lambda l, j: (0, 0)),
                  pl.BlockSpec((1, D_MODEL, MOD_TN), lambda l, j: (l, 0, j)),
                  pl.BlockSpec((1, 1, MOD_TN), lambda l, j: (l, 0, j))],
        out_specs=pl.BlockSpec((1, MOD_ROWS, MOD_TN), lambda l, j: (l, 0, j)),
        out_shape=jax.ShapeDtypeStruct((depth, MOD_ROWS, n), F32),
        compiler_params=_cparams(("parallel", "parallel")),
        name="adaln_mod",
    )(cc, w_ada, b_ada.reshape(depth, 1, n))


def _ffn_kernel(x_ref, mod_ref, wi_ref, wo_ref, g_ref, b_ref, o_ref, *, s):
    x = x_ref[0]
    mod = mod_ref[0]
    h = (_layer_norm(x) * (1.0 + mod[3 * s + 1:3 * s + 2]) + mod[3 * s:3 * s + 1]).astype(BF16)
    acc = jnp.zeros((x.shape[0], D_MODEL), F32)
    for c0, cw in FF_CHUNKS:
        gate = jnp.dot(h, wi_ref[:, c0:c0 + cw], preferred_element_type=F32)
        up = jnp.dot(h, wi_ref[:, D_FF + c0:D_FF + c0 + cw], preferred_element_type=F32)
        act = (gate * jax.nn.sigmoid(gate) * up).astype(BF16)
        acc = acc + jnp.dot(act, wo_ref[c0:c0 + cw, :], preferred_element_type=F32)
    y = ALPHA * x + (0.5 * mod[3 * s + 2:3 * s + 3]) * acc
    o_ref[0] = _layer_norm(y) * g_ref[...] + b_ref[...]


def _ffn(xs, mod, wi, wo, g, b, s):
    B, S, _ = xs.shape
    tile = pl.BlockSpec((1, TM, D_MODEL), lambda b_, j: (b_, j, 0))
    return pl.pallas_call(
        functools.partial(_ffn_kernel, s=s),
        grid=(B, S // TM),
        in_specs=[tile, _mod_spec(), _const_spec((D_MODEL, 2 * D_FF)), _const_spec((D_FF, D_MODEL)),
                  _const_spec((1, D_MODEL)), _const_spec((1, D_MODEL))],
        out_specs=tile,
        out_shape=jax.ShapeDtypeStruct(xs.shape, F32),
        compiler_params=_cparams(("parallel", "parallel")),
        name="macaron_ffn",
    )(xs, mod, wi, wo, g.reshape(1, D_MODEL), b.reshape(1, D_MODEL))


def _rope(x, cos, sin_signed, half):
    w = x.shape[1]
    reps = w // LANES
    if reps > 1:
        cos = jnp.concatenate([cos] * reps, axis=1)
        sin_signed = jnp.concatenate([sin_signed] * reps, axis=1)
    lane = lax.broadcasted_iota(jnp.int32, x.shape, 1)
    first = (lane % (2 * half)) < half
    partner = jnp.where(first, pltpu.roll(x, w - half, 1), pltpu.roll(x, half, 1))
    return x * cos + partner * sin_signed


def _with_ones(v, n_heads):
    ones = jnp.ones((v.shape[0], HEAD_W), v.dtype)
    parts = []
    for h in range(n_heads):
        parts += [v[:, h * HEAD_W:(h + 1) * HEAD_W], ones]
    return jnp.concatenate(parts, axis=1)


def _log_sigmoid(x):
    return jnp.minimum(x, 0.0) - jnp.log1p(jnp.exp(-jnp.abs(x)))


def _in_kernel(x_ref, mod_ref, w_ref, cos_a, sin_a, cos_d, sin_d, qg_ref, kg_ref, gb_ref, gm_ref,
               qa_ref, kat_ref, va_ref, u_ref, cq_ref, ck_ref, cv_ref, co_ref, gt_ref,
               qd_ref, kdt_ref, vd_ref):
    x = x_ref[0]
    mod = mod_ref[0]
    h = (_layer_norm(x) * (1.0 + mod[4:5]) + mod[3:4]).astype(BF16)
    y = jnp.dot(h, w_ref[...], preferred_element_type=F32)
    gw = GROUP_W
    ca, sa = cos_a[...], sin_a[...]
    qa = _rope(y[:, 0:gw], ca, sa, DA_QK // 2) * (DA_QK ** -0.5 * LOG2E)
    qa_ref[0] = qa.astype(BF16)
    ka = _rope(y[:, gw:2 * gw], ca, sa, DA_QK // 2)
    kat_ref[0] = ka.T.astype(BF16)
    va_ref[0] = _with_ones(y[:, 2 * gw:3 * gw], DA_HEADS).astype(BF16)
    u_ref[0] = y[:, 3 * gw:4 * gw]
    cq_ref[0] = (y[:, 4 * gw:5 * gw] * ML_DIM ** -0.5).astype(BF16)
    ck_ref[0] = y[:, 5 * gw:6 * gw].astype(BF16)
    cv_ref[0] = _with_ones(y[:, 6 * gw:7 * gw], ML_HEADS).astype(BF16)
    co_ref[0] = y[:, 7 * gw:8 * gw].astype(BF16)
    graw = y[:, 10 * gw:10 * gw + LANES] + gb_ref[...]
    lane = lax.broadcasted_iota(jnp.int32, graw.shape, 1)
    is_forget = ((lane // ML_HEADS) % 2) == 1
    gt_ref[0] = jnp.where(is_forget, _log_sigmoid(graw), graw)
    gm = gm_ref[...]
    cd, sd = cos_d[...], sin_d[...]
    qd = y[:, 8 * gw:9 * gw]
    qd = qd * lax.rsqrt(_group_mean(qd * qd, gm) + EPS) * qg_ref[...]
    qd_ref[0] = (_rope(qd, cd, sd, GQA_DIM // 2) * (GQA_DIM ** -0.5 * LOG2E)).astype(BF16)
    kvw = GQA_KV_HEADS * GQA_DIM
    kd = y[:, 9 * gw:9 * gw + kvw]
    kd = kd * lax.rsqrt(_group_mean(kd * kd, gm[:kvw, :kvw]) + EPS) * kg_ref[...]
    kdt_ref[0] = _rope(kd, cd, sd, GQA_DIM // 2).T.astype(BF16)
    vd_ref[0] = _with_ones(y[:, 9 * gw + kvw:9 * gw + 2 * kvw], GQA_KV_HEADS).astype(BF16)


def _in_proj(xs, mod, w_in, rope, qg, kg, gate_b, gm):
    B, S, _ = xs.shape
    gw = GROUP_W
    kvw = GQA_KV_HEADS * GQA_DIM

    def tok(width):
        return pl.BlockSpec((1, TM, width), lambda b, j: (b, j, 0))

    def tok_t(rows):
        return pl.BlockSpec((1, rows, TM), lambda b, j: (b, 0, j))

    def table():
        return pl.BlockSpec((TM, LANES), lambda b, j: (j, 0))

    out_shapes = [
        ((B, S, gw), BF16, tok(gw)),
        ((B, gw, S), BF16, tok_t(gw)),
        ((B, S, 2 * gw), BF16, tok(2 * gw)),
        ((B, S, gw), F32, tok(gw)),
        ((B, S, gw), BF16, tok(gw)),
        ((B, S, gw), BF16, tok(gw)),
        ((B, S, 2 * gw), BF16, tok(2 * gw)),
        ((B, S, gw), BF16, tok(gw)),
        ((B, S, LANES), F32, tok(LANES)),
        ((B, S, gw), BF16, tok(gw)),
        ((B, kvw, S), BF16, tok_t(kvw)),
        ((B, S, 2 * kvw), BF16, tok(2 * kvw)),
    ]
    return pl.pallas_call(
        _in_kernel,
        grid=(B, S // TM),
        in_specs=[tok(D_MODEL), _mod_spec(), _const_spec((D_MODEL, IN_W)),
                  table(), table(), table(), table(),
                  _const_spec((1, gw)), _const_spec((1, kvw)), _const_spec((1, LANES)),
                  _const_spec((gw, gw))],
        out_specs=[o[2] for o in out_shapes],
        out_shape=[jax.ShapeDtypeStruct(o[0], o[1]) for o in out_shapes],
        compiler_params=_cparams(("parallel", "parallel")),
        name="mix_in_proj",
    )(xs, mod, w_in, rope[0], rope[1], rope[2], rope[3], qg, kg, gate_b, gm)


def _flash(qs, kt_ref, k_rows, v_ref, v_col0, n_kv):
    rows = [q.shape[0] for q in qs]

    def body(i, carry):
        off = pl.multiple_of(i * TM, TM)
        vb = v_ref[0, pl.ds(off, TM), v_col0:v_col0 + LANES]
        out = []
        for q, (r0, d), (m, acc) in zip(qs, k_rows, carry):
            kt = kt_ref[0, r0:r0 + d, pl.ds(off, TM)]
            s = jnp.dot(q, kt, preferred_element_type=F32)
            m_new = jnp.maximum(m, jnp.max(s, axis=1, keepdims=True))
            p = jnp.exp2(s - m_new)
            acc = jnp.exp2(m - m_new) * acc + jnp.dot(p.astype(BF16), vb, preferred_element_type=F32)
            out.append((m_new, acc))
        return tuple(out)

    init = tuple((jnp.full((r, 1), -jnp.inf, F32), jnp.zeros((r, LANES), F32)) for r in rows)
    res = lax.fori_loop(0, n_kv, body, init)
    return [acc for _, acc in res]


def _n_kv_blocks(s_total):
    return jnp.where(pl.program_id(1) == 0, 1, s_total // TM)


def _diff_attn_kernel(q_ref, kt_ref, v_ref, lam_ref, g_ref, o_ref, *, lam_init):
    n_kv = _n_kv_blocks(kt_ref.shape[2])
    dl = lam_ref[...]
    lam = (jnp.exp(jnp.sum(dl[0:1] * dl[1:2], axis=1, keepdims=True))
           - jnp.exp(jnp.sum(dl[2:3] * dl[3:4], axis=1, keepdims=True)) + lam_init)
    q = q_ref[0]
    g = g_ref[...]
    outs = []
    for h in range(DA_HEADS):
        c0 = 2 * DA_QK * h
        qs = [q[:, c0:c0 + DA_QK], q[:, c0 + DA_QK:c0 + 2 * DA_QK]]
        acc1, acc2 = _flash(qs, kt_ref, [(c0, DA_QK), (c0 + DA_QK, DA_QK)], v_ref, LANES * h, n_kv)
        o = (acc1[:, :DA_V] / acc1[:, DA_V:DA_V + 1]
             - lam * (acc2[:, :DA_V] / acc2[:, DA_V:DA_V + 1]))
        o = o * lax.rsqrt(jnp.mean(o * o, axis=1, keepdims=True) + EPS)
        outs.append(o * g[:, DA_V * h:DA_V * (h + 1)] * (1.0 - lam_init))
    o_ref[0] = jnp.concatenate(outs, axis=1).astype(BF16)


def _gqa_kernel(q_ref, kt_ref, v_ref, o_ref):
    n_kv = _n_kv_blocks(kt_ref.shape[2])
    q = q_ref[0]
    group = GQA_HEADS // GQA_KV_HEADS
    outs = []
    for kvh in range(GQA_KV_HEADS):
        q0 = kvh * group * GQA_DIM
        qs = [q[:, q0 + g * GQA_DIM:q0 + (g + 1) * GQA_DIM] for g in range(group)]
        accs = _flash(qs, kt_ref, [(kvh * GQA_DIM, GQA_DIM)] * group, v_ref, LANES * kvh, n_kv)
        outs += [acc[:, :GQA_DIM] / acc[:, GQA_DIM:GQA_DIM + 1] for acc in accs]
    o_ref[0] = jnp.concatenate(outs, axis=1).astype(BF16)


def _attention(kernel_fn, q, kt, v, extra, name):
    B, S, gw = q.shape
    extra_specs = [_const_spec(e.shape) for e in extra]
    return pl.pallas_call(
        kernel_fn,
        grid=(B, S // TM),
        in_specs=[pl.BlockSpec((1, TM, gw), lambda b, j: (b, j, 0)),
                  pl.BlockSpec((1,) + kt.shape[1:], lambda b, j: (b, 0, 0)),
                  pl.BlockSpec((1,) + v.shape[1:], lambda b, j: (b, 0, 0))] + extra_specs,
        out_specs=pl.BlockSpec((1, TM, gw), lambda b, j: (b, j, 0)),
        out_shape=jax.ShapeDtypeStruct((B, S, gw), BF16),
        compiler_params=_cparams(("parallel", "arbitrary")),
        name=name,
    )(q, kt, v, *extra)


def _mlstm_direction(q, k, vaug, g, gt16, c_ref, m_ref, d, h_ref):
    T = q.shape[0]
    row = lax.broadcasted_iota(jnp.int32, (T, T), 0)
    col = lax.broadcasted_iota(jnp.int32, (T, T), 1)
    seen = (col <= row) if d == 0 else (col >= row)
    seen_f = seen.astype(F32)
    hi = lax.Precision.HIGHEST
    b_cols = jnp.dot(seen_f, g, preferred_element_type=F32, precision=hi)
    b_rows = lax.dot_general(gt16, seen_f, (((1,), (1,)), ((), ())),
                             preferred_element_type=F32, precision=hi)
    last = T - 1 if d == 0 else 0
    outs = []
    for h in range(ML_HEADS):
        li_c = 2 * ML_HEADS * d + h
        lf_c = li_c + ML_HEADS
        b_col = b_cols[:, lf_c:lf_c + 1]
        li_col = g[:, li_c:li_c + 1]
        r_row = gt16[li_c:li_c + 1, :] - b_rows[lf_c:lf_c + 1, :]
        m_prev = m_ref[d * ML_HEADS + h:d * ML_HEADS + h + 1, 0:1]
        c_prev = c_ref[d, h]
        qh = q[:, ML_DIM * h:ML_DIM * (h + 1)]
        kh = k[:, ML_DIM * h:ML_DIM * (h + 1)]
        vh = vaug[:, LANES * h:LANES * (h + 1)]
        dmat = jnp.where(seen, b_col + r_row, -jnp.inf)
        inter = b_col + m_prev
        m_t = jnp.maximum(inter, jnp.max(dmat, axis=1, keepdims=True))
        w_intra = jnp.exp(dmat - m_t)
        w_inter = jnp.exp(inter - m_t)
        s = lax.dot_general(qh, kh, (((1,), (1,)), ((), ())), preferred_element_type=F32)
        a = (s * w_intra).astype(BF16)
        num = (jnp.dot(a, vh, preferred_element_type=F32)
               + w_inter * jnp.dot(qh, c_prev.astype(BF16), preferred_element_type=F32))
        den = jnp.maximum(jnp.abs(num[:, ML_DIM:ML_DIM + 1]), jnp.exp(-m_t))
        outs.append(num[:, :ML_DIM] / den)
        b_end = b_col[last:last + 1, :]
        g_col = b_end - b_col + li_col
        m_new = jnp.maximum(b_end + m_prev, jnp.max(g_col, axis=0, keepdims=True))
        wk = jnp.exp(g_col - m_new)
        decay = jnp.exp(b_end + m_prev - m_new)
        kw = (kh.astype(F32) * wk).astype(BF16)
        c_ref[d, h] = decay * c_prev + lax.dot_general(kw, vh, (((0,), (0,)), ((), ())),
                                                       preferred_element_type=F32)
        m_ref[d * ML_HEADS + h:d * ML_HEADS + h + 1, :] = jnp.broadcast_to(m_new, (1, LANES))
    h_ref[0] = jnp.concatenate(outs, axis=1)


def _mlstm_kernel(qf_ref, kf_ref, vf_ref, gf_ref, qb_ref, kb_ref, vb_ref, gb_ref,
                  hf_ref, hb_ref, c_ref, m_ref):
    @pl.when(pl.program_id(1) == 0)
    def _():
        c_ref[...] = jnp.zeros_like(c_ref)
        m_ref[...] = jnp.zeros_like(m_ref)

    for d, (q_ref, k_ref, v_ref, g_ref, h_ref) in enumerate(
            ((qf_ref, kf_ref, vf_ref, gf_ref, hf_ref), (qb_ref, kb_ref, vb_ref, gb_ref, hb_ref))):
        g = g_ref[0]
        gt16 = g.T[0:4 * ML_HEADS, :]
        _mlstm_direction(q_ref[0], k_ref[0], v_ref[0], g, gt16, c_ref, m_ref, d, h_ref)


def _mlstm(cq, ck, cv, gates):
    B, S, gw = cq.shape
    nc = S // TM

    def fwd(width):
        return pl.BlockSpec((1, TM, width), lambda b, i: (b, i, 0))

    def bwd(width):
        return pl.BlockSpec((1, TM, width), lambda b, i: (b, jnp.where(i == 0, 0, nc - i), 0))

    widths = (gw, gw, 2 * gw, LANES)
    return pl.pallas_call(
        _mlstm_kernel,
        grid=(B, nc),
        in_specs=[fwd(w) for w in widths] + [bwd(w) for w in widths],
        out_specs=[fwd(gw), bwd(gw)],
        out_shape=[jax.ShapeDtypeStruct((B, S, gw), F32)] * 2,
        scratch_shapes=[pltpu.VMEM((2, ML_HEADS, ML_DIM, LANES), F32),
                        pltpu.VMEM((2 * ML_HEADS, LANES), F32)],
        compiler_params=_cparams(("parallel", "arbitrary")),
        name="mlstm_scan",
    )(cq, ck, cv, gates, cq, ck, cv, gates)


def _pool_mixer(u_ref, up_ref, un_ref, ext_ref, pw_ref, ps_ref):
    j = pl.program_id(1)
    nt = pl.num_programs(1)
    has_prev = j >= 2
    has_next = jnp.logical_and(j >= 1, j < nt - 1)
    u = u_ref[0]
    ext_ref[0:POOL_HALO, :] = jnp.where(has_prev, up_ref[0], 0.0)
    ext_ref[POOL_HALO:POOL_HALO + TM, :] = u
    ext_ref[POOL_HALO + TM:, :] = jnp.where(has_next, un_ref[0], 0.0)

    def shifted(s):
        return ext_ref[POOL_HALO + s:POOL_HALO + s + TM, :]

    lane = lax.broadcasted_iota(jnp.int32, (TM, GROUP_W), 1)
    r = lax.broadcasted_iota(jnp.int32, (TM, GROUP_W), 0)
    far = 2 * POOL_HALO
    left_room = r + jnp.where(has_prev, far, 0)
    right_room = (TM - 1 - r) + jnp.where(has_next, far, 0)
    total = u
    mean = jnp.zeros_like(u)
    prev_half = 0
    for gi, w in enumerate(POOL_WINDOWS):
        half = w // 2
        for s in list(range(-half, -prev_half)) + list(range(max(prev_half, 1), half)):
            total = total + shifted(s)
        prev_half = half
        cnt = jnp.minimum(left_room, half) + jnp.minimum(right_room, half - 1) + 1
        mean = jnp.where(lane // POOL_GROUP == gi, total / cnt.astype(F32), mean)
    dlt = (mean - u).astype(BF16)
    return jnp.dot(dlt, pw_ref[...], preferred_element_type=F32) * ps_ref[...]


def _out_kernel(x_ref, mod_ref, a_ref, u_ref, up_ref, un_ref, hf_ref, hb_ref, co_ref, d_ref,
                w_ref, pw_ref, ps_ref, mg_ref, gm_ref, g_ref, b_ref, o_ref, ext_ref):
    gw = GROUP_W
    x = x_ref[0]
    mod = mod_ref[0]
    b_mix = _pool_mixer(u_ref, up_ref, un_ref, ext_ref, pw_ref, ps_ref)
    hm = hf_ref[0] + hb_ref[0]
    c_mix = (hm * lax.rsqrt(_group_mean(hm * hm, gm_ref[...]) + EPS) * mg_ref[...]
             * jax.nn.sigmoid(co_ref[0].astype(F32)))
    o = (jnp.dot(a_ref[0], w_ref[0:gw, :], preferred_element_type=F32)
         + jnp.dot(b_mix.astype(BF16), w_ref[gw:2 * gw, :], preferred_element_type=F32)
         + jnp.dot(c_mix.astype(BF16), w_ref[2 * gw:3 * gw, :], preferred_element_type=F32)
         + jnp.dot(d_ref[0], w_ref[3 * gw:4 * gw, :], preferred_element_type=F32))
    y = ALPHA * x + mod[5:6] * o
    o_ref[0] = _layer_norm(y) * g_ref[...] + b_ref[...]


def _out_proj(xs, mod, a, u, hf, hb, co, dd, w_out, pool_bd, pool_scale, ml_g, gm, g, b):
    B, S, _ = xs.shape
    gw = GROUP_W
    hb_per_tile = TM // POOL_HALO
    n_halo = S // POOL_HALO

    def tok(width):
        return pl.BlockSpec((1, TM, width), lambda b_, j: (b_, j, 0))

    prev_spec = pl.BlockSpec((1, POOL_HALO, gw),
                             lambda b_, j: (b_, jnp.maximum(j * hb_per_tile - 1, 0), 0))
    next_spec = pl.BlockSpec((1, POOL_HALO, gw),
                             lambda b_, j: (b_, jnp.minimum((j + 1) * hb_per_tile, n_halo - 1), 0))
    return pl.pallas_call(
        _out_kernel,
        grid=(B, S // TM),
        in_specs=[tok(D_MODEL), _mod_spec(), tok(gw), tok(gw), prev_spec, next_spec,
                  tok(gw), tok(gw), tok(gw), tok(gw),
                  _const_spec((D_MODEL, D_MODEL)), _const_spec((gw, gw)), _const_spec((1, gw)),
                  _const_spec((1, gw)), _const_spec((gw, gw)),
                  _const_spec((1, D_MODEL)), _const_spec((1, D_MODEL))],
        out_specs=tok(D_MODEL),
        out_shape=jax.ShapeDtypeStruct(xs.shape, F32),
        scratch_shapes=[pltpu.VMEM((TM + 2 * POOL_HALO, gw), F32)],
        compiler_params=_cparams(("parallel", "parallel")),
        name="mix_out_proj",
    )(xs, mod, a, u, u, u, hf, hb, co, dd, w_out, pool_bd, pool_scale, ml_g, gm, g, b)


def _rope_tables(L, Lc, dim):
    rows = L // GRID_W
    row = jnp.repeat(jnp.arange(rows), GRID_W).astype(F32)
    col = jnp.tile(jnp.arange(GRID_W), rows).astype(F32)
    axis_dim = dim // 2
    inv = ROPE_THETA ** (-jnp.arange(0, axis_dim, 2, dtype=F32) / axis_dim)
    ang = jnp.concatenate([row[:, None] * inv, col[:, None] * inv], axis=-1)
    cos, sin = jnp.cos(ang), jnp.sin(ang)
    reps = LANES // dim
    cos_p = jnp.tile(jnp.concatenate([cos, cos], axis=-1), (1, reps))
    sin_p = jnp.tile(jnp.concatenate([-sin, sin], axis=-1), (1, reps))
    cos_p = jnp.concatenate([jnp.ones((Lc, LANES), F32), cos_p], axis=0)
    sin_p = jnp.concatenate([jnp.zeros((Lc, LANES), F32), sin_p], axis=0)
    return cos_p, sin_p


def _reorder_w_in(w):
    g0 = 8 * GROUP_W
    g1 = g0 + 4 * ML_HEADS
    pad = jnp.zeros((w.shape[0], LANES - 4 * ML_HEADS), w.dtype)
    return jnp.concatenate([w[:, :g0], w[:, g1:], w[:, g0:g1], pad], axis=1).astype(BF16)


def _block_diag(blocks):
    n = blocks.shape[0]
    rows = []
    for i in range(n):
        rows.append(jnp.concatenate([blocks[i] if k == i else jnp.zeros_like(blocks[i])
                                     for k in range(n)], axis=1))
    return jnp.concatenate(rows, axis=0)


def kernel(x, c, ctx, c_ctx, w_ada, b_ada, ln_g, ln_b, ffn1_wi, ffn1_wo, ffn2_wi, ffn2_wo, w_in, w_out,
           diff_lambda, diff_norm_g, pool_w, pool_scale, ml_gate_b, ml_norm_g, gqa_qnorm_g, gqa_knorm_g):
    B, L, _ = x.shape
    Lc = ctx.shape[1]
    assert Lc == TM and L % TM == 0 and L % GRID_W == 0 and B + 1 <= MOD_ROWS
    depth = w_ada.shape[0]

    cc = jnp.concatenate([c_ctx[None], c, jnp.zeros((MOD_ROWS - 1 - B, D_MODEL), F32)], axis=0)
    mod_all = _modulation(cc, w_ada, b_ada).reshape(depth, MOD_ROWS, N_MOD, D_MODEL)

    rope = _rope_tables(L, Lc, DA_QK) + _rope_tables(L, Lc, GQA_DIM)
    gm = _block_diag(jnp.full((GROUP_W // HEAD_W, HEAD_W, HEAD_W), 1.0 / HEAD_W, F32))

    xs = jnp.concatenate([ctx, x], axis=1)
    for l in range(depth):
        mod = mod_all[l]
        lam_init = 0.8 - 0.6 * math.exp(-0.3 * l)
        xs = _ffn(xs, mod, ffn1_wi[l].astype(BF16), ffn1_wo[l].astype(BF16), ln_g[l, 0], ln_b[l, 0], 0)
        gate_b = jnp.concatenate([ml_gate_b[l].reshape(1, -1),
                                  jnp.zeros((1, LANES - 4 * ML_HEADS), F32)], axis=1)
        qg = jnp.tile(gqa_qnorm_g[l], GQA_HEADS).reshape(1, -1)
        kg = jnp.tile(gqa_knorm_g[l], GQA_KV_HEADS).reshape(1, -1)
        (qa, kat, va, u, cq, ck, cv, co, gates, qd, kdt, vd) = _in_proj(
            xs, mod, _reorder_w_in(w_in[l]), rope, qg, kg, gate_b, gm)
        a = _attention(functools.partial(_diff_attn_kernel, lam_init=lam_init), qa, kat, va,
                       [diff_lambda[l], diff_norm_g[l].reshape(1, -1)], "diff_attention")
        dd = _attention(_gqa_kernel, qd, kdt, vd, [], "gqa_attention")
        hf, hb = _mlstm(cq, ck, cv, gates)
        xs = _out_proj(xs, mod, a, u, hf, hb, co, dd, w_out[l].astype(BF16),
                       _block_diag(pool_w[l]).astype(BF16), pool_scale[l].reshape(1, -1),
                       ml_norm_g[l].reshape(1, -1), gm, ln_g[l, 1].reshape(1, -1),
                       ln_b[l, 1].reshape(1, -1))
        xs = _ffn(xs, mod, ffn2_wi[l].astype(BF16), ffn2_wo[l].astype(BF16), ln_g[l, 2], ln_b[l, 2], 2)
    return xs[:, Lc:]
```

```python
import functools
import math

import jax
import jax.numpy as jnp
from jax import lax
from jax.experimental import pallas as pl
from jax.experimental.pallas import tpu as pltpu

F32 = jnp.float32
BF16 = jnp.bfloat16

D_MODEL = 1024
DEPTH = 2
GRID_W = 64
GROUP_W = 256
D_FF = 2816
N_MOD = 9
EPS = 1e-6
ROPE_THETA = 10000.0
DA_HEADS = 4
DA_QK = 32
DA_V = 64
POOL_WINDOWS = (2, 4, 8, 16)
POOL_GROUP = 64
POOL_HALO = 8
ML_HEADS = 4
ML_DIM = 64
GQA_HEADS = 4
GQA_KV_HEADS = 2
GQA_DIM = 64
HEAD_W = 64
LANES = 128
ALPHA = (2.0 * DEPTH) ** 0.25
LOG2E = math.log2(math.e)

TM = 256
KV_BLOCK = 512
FF_CHUNKS = ((0, 1024), (1024, 1024), (2048, 768))
IN_W = 2688
MOD_ROWS = 16
MOD_TN = 1152
VMEM_LIMIT = 56 * 1024 * 1024


def _cparams(sem):
    return pltpu.CompilerParams(dimension_semantics=sem, vmem_limit_bytes=VMEM_LIMIT)


def _const_spec(shape):
    nd = len(shape)
    return pl.BlockSpec(shape, lambda b, j: (0,) * nd, pipeline_mode=pl.Buffered(1))


def _mod_spec():
    return pl.BlockSpec((1, N_MOD, D_MODEL), lambda b, j: (jnp.where(j == 0, 0, b + 1), 0, 0))


def _layer_norm(x):
    mu = jnp.mean(x, axis=-1, keepdims=True)
    xc = x - mu
    var = jnp.mean(xc * xc, axis=-1, keepdims=True)
    return xc * lax.rsqrt(var + EPS)


def _group_mean(x, gm):
    return jnp.dot(x, gm, preferred_element_type=F32, precision=lax.Precision.HIGHEST)


def _mod_kernel(c_ref, w_ref, b_ref, o_ref):
    c = c_ref[...]
    s = c * jax.nn.sigmoid(c)
    o_ref[0] = jnp.dot(s, w_ref[0], preferred_element_type=F32,
                       precision=lax.Precision.HIGHEST) + b_ref[0]


def _modulation(cc, w_ada, b_ada):
    depth = w_ada.shape[0]
    n = N_MOD * D_MODEL
    return pl.pallas_call(
        _mod_kernel,
        grid=(depth, n // MOD_TN),
        in_specs=[pl.BlockSpec((MOD_ROWS, D_MODEL), lambda l, j: (0, 0)),
                  pl.BlockSpec((1, D_MODEL, MOD_TN), lambda l, j: (l, 0, j)),
                  pl.BlockSpec((1, 1, MOD_TN), lambda l, j: (l, 0, j))],
        out_specs=pl.BlockSpec((1, MOD_ROWS, MOD_TN), lambda l, j: (l, 0, j)),
        out_shape=jax.ShapeDtypeStruct((depth, MOD_ROWS, n), F32),
        compiler_params=_cparams(("parallel", "parallel")),
        name="adaln_mod",
    )(cc, w_ada, b_ada.reshape(depth, 1, n))


def _ffn_kernel(x_ref, mod_ref, wi_ref, wo_ref, g_ref, b_ref, o_ref, *, s):
    x = x_ref[0]
    mod = mod_ref[0]
    h = (_layer_norm(x) * (1.0 + mod[3 * s + 1:3 * s + 2]) + mod[3 * s:3 * s + 1]).astype(BF16)
    acc = jnp.zeros((x.shape[0], D_MODEL), F32)
    for c0, cw in FF_CHUNKS:
        gate = jnp.dot(h, wi_ref[:, c0:c0 + cw], preferred_element_type=F32)
        up = jnp.dot(h, wi_ref[:, D_FF + c0:D_FF + c0 + cw], preferred_element_type=F32)
        act = (gate * jax.nn.sigmoid(gate) * up).astype(BF16)
        acc = acc + jnp.dot(act, wo_ref[c0:c0 + cw, :], preferred_element_type=F32)
    y = ALPHA * x + (0.5 * mod[3 * s + 2:3 * s + 3]) * acc
    o_ref[0] = _layer_norm(y) * g_ref[...] + b_ref[...]


def _ffn(xs, mod, wi, wo, g, b, s):
    B, S, _ = xs.shape
    tile = pl.BlockSpec((1, TM, D_MODEL), lambda b_, j: (b_, j, 0))
    return pl.pallas_call(
        functools.partial(_ffn_kernel, s=s),
        grid=(B, S // TM),
        in_specs=[tile, _mod_spec(), _const_spec((D_MODEL, 2 * D_FF)), _const_spec((D_FF, D_MODEL)),
                  _const_spec((1, D_MODEL)), _const_spec((1, D_MODEL))],
        out_specs=tile,
        out_shape=jax.ShapeDtypeStruct(xs.shape, F32),
        compiler_params=_cparams(("parallel", "parallel")),
        name="macaron_ffn",
    )(xs, mod, wi, wo, g.reshape(1, D_MODEL), b.reshape(1, D_MODEL))


def _rope(x, cos, sin_signed, half):
    w = x.shape[1]
    reps = w // LANES
    if reps > 1:
        cos = jnp.concatenate([cos] * reps, axis=1)
        sin_signed = jnp.concatenate([sin_signed] * reps, axis=1)
    lane = lax.broadcasted_iota(jnp.int32, x.shape, 1)
    first = (lane % (2 * half)) < half
    partner = jnp.where(first, pltpu.roll(x, w - half, 1), pltpu.roll(x, half, 1))
    return x * cos + partner * sin_signed


def _with_ones(v, n_heads):
    ones = jnp.ones((v.shape[0], HEAD_W), v.dtype)
    parts = []
    for h in range(n_heads):
        parts += [v[:, h * HEAD_W:(h + 1) * HEAD_W], ones]
    return jnp.concatenate(parts, axis=1)


def _log_sigmoid(x):
    return jnp.minimum(x, 0.0) - jnp.log1p(jnp.exp(-jnp.abs(x)))


def _values_t_with_ones(v, n_heads):
    vt = v.T
    ones = jnp.ones((HEAD_W, v.shape[0]), v.dtype)
    parts = []
    for h in range(n_heads):
        parts += [vt[h * HEAD_W:(h + 1) * HEAD_W], ones]
    return jnp.concatenate(parts, axis=0)


def _in_kernel(x_ref, mod_ref, w_ref, cos_a, sin_a, cos_d, sin_d, qg_ref, kg_ref, gb_ref, gm_ref,
               qat_ref, ka_ref, vat_ref, u_ref, cq_ref, ck_ref, cv_ref, co_ref, gt_ref,
               qdt_ref, kd_ref, vdt_ref):
    x = x_ref[0]
    mod = mod_ref[0]
    h = (_layer_norm(x) * (1.0 + mod[4:5]) + mod[3:4]).astype(BF16)
    y = jnp.dot(h, w_ref[...], preferred_element_type=F32)
    gw = GROUP_W
    ca, sa = cos_a[...], sin_a[...]
    qa = _rope(y[:, 0:gw], ca, sa, DA_QK // 2) * (DA_QK ** -0.5 * LOG2E)
    qat_ref[0] = qa.T.astype(BF16)
    ka_ref[0] = _rope(y[:, gw:2 * gw], ca, sa, DA_QK // 2).astype(BF16)
    vat_ref[0] = _values_t_with_ones(y[:, 2 * gw:3 * gw], DA_HEADS).astype(BF16)
    u_ref[0] = y[:, 3 * gw:4 * gw]
    cq_ref[0] = (y[:, 4 * gw:5 * gw] * ML_DIM ** -0.5).astype(BF16)
    ck_ref[0] = y[:, 5 * gw:6 * gw].astype(BF16)
    cv_ref[0] = _with_ones(y[:, 6 * gw:7 * gw], ML_HEADS).astype(BF16)
    co_ref[0] = y[:, 7 * gw:8 * gw].astype(BF16)
    graw = y[:, 10 * gw:10 * gw + LANES] + gb_ref[...]
    lane = lax.broadcasted_iota(jnp.int32, graw.shape, 1)
    is_forget = ((lane // ML_HEADS) % 2) == 1
    gt_ref[0] = jnp.where(is_forget, _log_sigmoid(graw), graw)
    gm = gm_ref[...]
    cd, sd = cos_d[...], sin_d[...]
    qd = y[:, 8 * gw:9 * gw]
    qd = qd * lax.rsqrt(_group_mean(qd * qd, gm) + EPS) * qg_ref[...]
    qdt_ref[0] = (_rope(qd, cd, sd, GQA_DIM // 2) * (GQA_DIM ** -0.5 * LOG2E)).T.astype(BF16)
    kvw = GQA_KV_HEADS * GQA_DIM
    kd = y[:, 9 * gw:9 * gw + kvw]
    kd = kd * lax.rsqrt(_group_mean(kd * kd, gm[:kvw, :kvw]) + EPS) * kg_ref[...]
    kd_ref[0] = _rope(kd, cd, sd, GQA_DIM // 2).astype(BF16)
    vdt_ref[0] = _values_t_with_ones(y[:, 9 * gw + kvw:9 * gw + 2 * kvw], GQA_KV_HEADS).astype(BF16)


def _in_proj(xs, mod, w_in, rope, qg, kg, gate_b, gm):
    B, S, _ = xs.shape
    gw = GROUP_W
    kvw = GQA_KV_HEADS * GQA_DIM

    def tok(width):
        return pl.BlockSpec((1, TM, width), lambda b, j: (b, j, 0))

    def tok_t(rows):
        return pl.BlockSpec((1, rows, TM), lambda b, j: (b, 0, j))

    def table():
        return pl.BlockSpec((TM, LANES), lambda b, j: (j, 0))

    out_shapes = [
        ((B, gw, S), BF16, tok_t(gw)),
        ((B, S, gw), BF16, tok(gw)),
        ((B, 2 * gw, S), BF16, tok_t(2 * gw)),
        ((B, S, gw), F32, tok(gw)),
        ((B, S, gw), BF16, tok(gw)),
        ((B, S, gw), BF16, tok(gw)),
        ((B, S, 2 * gw), BF16, tok(2 * gw)),
        ((B, S, gw), BF16, tok(gw)),
        ((B, S, LANES), F32, tok(LANES)),
        ((B, gw, S), BF16, tok_t(gw)),
        ((B, S, kvw), BF16, tok(kvw)),
        ((B, 2 * kvw, S), BF16, tok_t(2 * kvw)),
    ]
    return pl.pallas_call(
        _in_kernel,
        grid=(B, S // TM),
        in_specs=[tok(D_MODEL), _mod_spec(), _const_spec((D_MODEL, IN_W)),
                  table(), table(), table(), table(),
                  _const_spec((1, gw)), _const_spec((1, kvw)), _const_spec((1, LANES)),
                  _const_spec((gw, gw))],
        out_specs=[o[2] for o in out_shapes],
        out_shape=[jax.ShapeDtypeStruct(o[0], o[1]) for o in out_shapes],
        compiler_params=_cparams(("parallel", "parallel")),
        name="mix_in_proj",
    )(xs, mod, w_in, rope[0], rope[1], rope[2], rope[3], qg, kg, gate_b, gm)


def _flash_block(k_blk, vt_blks, qw_ref, acc_ref, m_ref, first):
    sts = [jnp.dot(k_blk, qw_ref[c], preferred_element_type=F32) for c in range(len(vt_blks))]
    for c, vt in enumerate(vt_blks):
        st = sts[c]
        m_blk = jnp.max(st, axis=0, keepdims=True)
        if first:
            m_new = m_blk
        else:
            m_old = m_ref[c]
            m_new = jnp.maximum(m_old, m_blk)
        p = jnp.exp2((st - m_new).astype(BF16))
        pv = jnp.dot(vt, p, preferred_element_type=F32)
        if first:
            acc_ref[c] = pv
        else:
            acc_ref[c] = jnp.exp2(m_old - m_new) * acc_ref[c] + pv
        m_ref[c] = m_new


def _flash_sweep(k_ref, vt_ref, v_row_of_chain, qw_ref, acc_ref, m_ref):
    s_total = k_ref.shape[1]

    def vt_blocks(start, size):
        return [vt_ref[0, r0:r0 + LANES, pl.ds(start, size)] for r0 in v_row_of_chain]

    _flash_block(k_ref[0, 0:TM, :], vt_blocks(0, TM), qw_ref, acc_ref, m_ref, True)

    @pl.when(pl.program_id(1) > 0)
    def _():
        def body(i, _):
            start = pl.multiple_of(TM + i * KV_BLOCK, TM)
            _flash_block(k_ref[0, pl.ds(start, KV_BLOCK), :], vt_blocks(start, KV_BLOCK),
                         qw_ref, acc_ref, m_ref, False)
            return 0

        lax.fori_loop(0, (s_total - TM) // KV_BLOCK, body, 0)


def _attn_out_t(acc):
    return acc[:HEAD_W] * (1.0 / acc[HEAD_W:HEAD_W + 1])


def _diff_attn_kernel(qt_ref, k_ref, vt_ref, lam_ref, g_ref, o_ref, qw_ref, acc_ref, m_ref, *,
                      lam_init):
    qt = qt_ref[0]
    feat = lax.broadcasted_iota(jnp.int32, qt.shape, 0)
    n_chains = 2 * DA_HEADS
    for c in range(n_chains):
        qw_ref[c] = jnp.where(feat // DA_QK == c, qt, jnp.zeros_like(qt))
    _flash_sweep(k_ref, vt_ref, [LANES * (c // 2) for c in range(n_chains)], qw_ref, acc_ref, m_ref)
    dl = lam_ref[...]
    lam = (jnp.exp(jnp.sum(dl[0:1] * dl[1:2], axis=1, keepdims=True))
           - jnp.exp(jnp.sum(dl[2:3] * dl[3:4], axis=1, keepdims=True)) + lam_init)
    outs = []
    for h in range(DA_HEADS):
        o = _attn_out_t(acc_ref[2 * h]) - lam * _attn_out_t(acc_ref[2 * h + 1])
        outs.append(o * lax.rsqrt(jnp.mean(o * o, axis=0, keepdims=True) + EPS))
    o_nat = jnp.concatenate(outs, axis=0).T
    o_ref[0] = (o_nat * g_ref[...] * (1.0 - lam_init)).astype(BF16)


def _gqa_kernel(qt_ref, k_ref, vt_ref, o_ref, qw_ref, acc_ref, m_ref):
    qt = qt_ref[0]
    group = GQA_HEADS // GQA_KV_HEADS
    zeros = jnp.zeros((GQA_DIM, qt.shape[1]), qt.dtype)
    for qh in range(GQA_HEADS):
        q_rows = qt[qh * GQA_DIM:(qh + 1) * GQA_DIM]
        kvh = qh // group
        qw_ref[qh] = jnp.concatenate([q_rows if i == kvh else zeros for i in range(GQA_KV_HEADS)],
                                     axis=0)
    _flash_sweep(k_ref, vt_ref, [LANES * (qh // group) for qh in range(GQA_HEADS)],
                 qw_ref, acc_ref, m_ref)
    o_t = jnp.concatenate([_attn_out_t(acc_ref[qh]) for qh in range(GQA_HEADS)], axis=0)
    o_ref[0] = o_t.T.astype(BF16)


def _attention(kernel_fn, qt, k, vt, extra, n_chains, name):
    B, S, kw = k.shape
    gw = qt.shape[1]
    extra_specs = [_const_spec(e.shape) for e in extra]
    return pl.pallas_call(
        kernel_fn,
        grid=(B, S // TM),
        in_specs=[pl.BlockSpec((1, gw, TM), lambda b, j: (b, 0, j)),
                  pl.BlockSpec((1, S, kw), lambda b, j: (b, 0, 0)),
                  pl.BlockSpec((1,) + vt.shape[1:], lambda b, j: (b, 0, 0))] + extra_specs,
        out_specs=pl.BlockSpec((1, TM, gw), lambda b, j: (b, j, 0)),
        out_shape=jax.ShapeDtypeStruct((B, S, gw), BF16),
        scratch_shapes=[pltpu.VMEM((n_chains, kw, TM), BF16),
                        pltpu.VMEM((n_chains, LANES, TM), F32),
                        pltpu.VMEM((n_chains, 1, TM), F32)],
        compiler_params=_cparams(("parallel", "arbitrary")),
        name=name,
    )(qt, k, vt, *extra)


def _mlstm_direction(q, k, vaug, g, gt16, c_ref, m_ref, d, h_ref):
    T = q.shape[0]
    row = lax.broadcasted_iota(jnp.int32, (T, T), 0)
    col = lax.broadcasted_iota(jnp.int32, (T, T), 1)
    seen = (col <= row) if d == 0 else (col >= row)
    seen_f = seen.astype(F32)
    hi = lax.Precision.HIGHEST
    b_cols = jnp.dot(seen_f, g, preferred_element_type=F32, precision=hi)
    b_rows = lax.dot_general(gt16, seen_f, (((1,), (1,)), ((), ())),
                             preferred_element_type=F32, precision=hi)
    last = T - 1 if d == 0 else 0
    outs = []
    for h in range(ML_HEADS):
        li_c = 2 * ML_HEADS * d + h
        lf_c = li_c + ML_HEADS
        b_col = b_cols[:, lf_c:lf_c + 1]
        li_col = g[:, li_c:li_c + 1]
        r_row = gt16[li_c:li_c + 1, :] - b_rows[lf_c:lf_c + 1, :]
        m_prev = m_ref[d * ML_HEADS + h:d * ML_HEADS + h + 1, 0:1]
        c_prev = c_ref[d, h]
        qh = q[:, ML_DIM * h:ML_DIM * (h + 1)]
        kh = k[:, ML_DIM * h:ML_DIM * (h + 1)]
        vh = vaug[:, LANES * h:LANES * (h + 1)]
        dmat = jnp.where(seen, b_col + r_row, -jnp.inf)
        inter = b_col + m_prev
        m_t = jnp.maximum(inter, jnp.max(dmat, axis=1, keepdims=True))
        w_intra = jnp.exp(dmat - m_t)
        w_inter = jnp.exp(inter - m_t)
        s = lax.dot_general(qh, kh, (((1,), (1,)), ((), ())), preferred_element_type=F32)
        a = (s * w_intra).astype(BF16)
        num = (jnp.dot(a, vh, preferred_element_type=F32)
               + w_inter * jnp.dot(qh, c_prev.astype(BF16), preferred_element_type=F32))
        den = jnp.maximum(jnp.abs(num[:, ML_DIM:ML_DIM + 1]), jnp.exp(-m_t))
        outs.append(num[:, :ML_DIM] / den)
        b_end = b_col[last:last + 1, :]
        g_col = b_end - b_col + li_col
        m_new = jnp.maximum(b_end + m_prev, jnp.max(g_col, axis=0, keepdims=True))
        wk = jnp.exp(g_col - m_new)
        decay = jnp.exp(b_end + m_prev - m_new)
        kw = (kh.astype(F32) * wk).astype(BF16)
        c_ref[d, h] = decay * c_prev + lax.dot_general(kw, vh, (((0,), (0,)), ((), ())),
                                                       preferred_element_type=F32)
        m_ref[d * ML_HEADS + h:d * ML_HEADS + h + 1, :] = jnp.broadcast_to(m_new, (1, LANES))
    h_ref[0] = jnp.concatenate(outs, axis=1)


def _mlstm_kernel(qf_ref, kf_ref, vf_ref, gf_ref, qb_ref, kb_ref, vb_ref, gb_ref,
                  hf_ref, hb_ref, c_ref, m_ref):
    @pl.when(pl.program_id(1) == 0)
    def _():
        c_ref[...] = jnp.zeros_like(c_ref)
        m_ref[...] = jnp.zeros_like(m_ref)

    for d, (q_ref, k_ref, v_ref, g_ref, h_ref) in enumerate(
            ((qf_ref, kf_ref, vf_ref, gf_ref, hf_ref), (qb_ref, kb_ref, vb_ref, gb_ref, hb_ref))):
        g = g_ref[0]
        gt16 = g.T[0:4 * ML_HEADS, :]
        _mlstm_direction(q_ref[0], k_ref[0], v_ref[0], g, gt16, c_ref, m_ref, d, h_ref)


def _mlstm(cq, ck, cv, gates):
    B, S, gw = cq.shape
    nc = S // TM

    def fwd(width):
        return pl.BlockSpec((1, TM, width), lambda b, i: (b, i, 0))

    def bwd(width):
        return pl.BlockSpec((1, TM, width), lambda b, i: (b, jnp.where(i == 0, 0, nc - i), 0))

    widths = (gw, gw, 2 * gw, LANES)
    return pl.pallas_call(
        _mlstm_kernel,
        grid=(B, nc),
        in_specs=[fwd(w) for w in widths] + [bwd(w) for w in widths],
        out_specs=[fwd(gw), bwd(gw)],
        out_shape=[jax.ShapeDtypeStruct((B, S, gw), F32)] * 2,
        scratch_shapes=[pltpu.VMEM((2, ML_HEADS, ML_DIM, LANES), F32),
                        pltpu.VMEM((2 * ML_HEADS, LANES), F32)],
        compiler_params=_cparams(("parallel", "arbitrary")),
        name="mlstm_scan",
    )(cq, ck, cv, gates, cq, ck, cv, gates)


def _pool_mixer(u_ref, up_ref, un_ref, ext_ref, pw_ref, ps_ref):
    j = pl.program_id(1)
    nt = pl.num_programs(1)
    has_prev = j >= 2
    has_next = jnp.logical_and(j >= 1, j < nt - 1)
    u = u_ref[0]
    ext_ref[0:POOL_HALO, :] = jnp.where(has_prev, up_ref[0], 0.0)
    ext_ref[POOL_HALO:POOL_HALO + TM, :] = u
    ext_ref[POOL_HALO + TM:, :] = jnp.where(has_next, un_ref[0], 0.0)

    def shifted(s):
        return ext_ref[POOL_HALO + s:POOL_HALO + s + TM, :]

    lane = lax.broadcasted_iota(jnp.int32, (TM, GROUP_W), 1)
    r = lax.broadcasted_iota(jnp.int32, (TM, GROUP_W), 0)
    far = 2 * POOL_HALO
    left_room = r + jnp.where(has_prev, far, 0)
    right_room = (TM - 1 - r) + jnp.where(has_next, far, 0)
    total = u
    mean = jnp.zeros_like(u)
    prev_half = 0
    for gi, w in enumerate(POOL_WINDOWS):
        half = w // 2
        for s in list(range(-half, -prev_half)) + list(range(max(prev_half, 1), half)):
            total = total + shifted(s)
        prev_half = half
        cnt = jnp.minimum(left_room, half) + jnp.minimum(right_room, half - 1) + 1
        mean = jnp.where(lane // POOL_GROUP == gi, total / cnt.astype(F32), mean)
    dlt = (mean - u).astype(BF16)
    return jnp.dot(dlt, pw_ref[...], preferred_element_type=F32) * ps_ref[...]


def _out_kernel(x_ref, mod_ref, a_ref, u_ref, up_ref, un_ref, hf_ref, hb_ref, co_ref, d_ref,
                w_ref, pw_ref, ps_ref, mg_ref, gm_ref, g_ref, b_ref, o_ref, ext_ref):
    gw = GROUP_W
    x = x_ref[0]
    mod = mod_ref[0]
    b_mix = _pool_mixer(u_ref, up_ref, un_ref, ext_ref, pw_ref, ps_ref)
    hm = hf_ref[0] + hb_ref[0]
    c_mix = (hm * lax.rsqrt(_group_mean(hm * hm, gm_ref[...]) + EPS) * mg_ref[...]
             * jax.nn.sigmoid(co_ref[0].astype(F32)))
    o = (jnp.dot(a_ref[0], w_ref[0:gw, :], preferred_element_type=F32)
         + jnp.dot(b_mix.astype(BF16), w_ref[gw:2 * gw, :], preferred_element_type=F32)
         + jnp.dot(c_mix.astype(BF16), w_ref[2 * gw:3 * gw, :], preferred_element_type=F32)
         + jnp.dot(d_ref[0], w_ref[3 * gw:4 * gw, :], preferred_element_type=F32))
    y = ALPHA * x + mod[5:6] * o
    o_ref[0] = _layer_norm(y) * g_ref[...] + b_ref[...]


def _out_proj(xs, mod, a, u, hf, hb, co, dd, w_out, pool_bd, pool_scale, ml_g, gm, g, b):
    B, S, _ = xs.shape
    gw = GROUP_W
    hb_per_tile = TM // POOL_HALO
    n_halo = S // POOL_HALO

    def tok(width):
        return pl.BlockSpec((1, TM, width), lambda b_, j: (b_, j, 0))

    prev_spec = pl.BlockSpec((1, POOL_HALO, gw),
                             lambda b_, j: (b_, jnp.maximum(j * hb_per_tile - 1, 0), 0))
    next_spec = pl.BlockSpec((1, POOL_HALO, gw),
                             lambda b_, j: (b_, jnp.minimum((j + 1) * hb_per_tile, n_halo - 1), 0))
    return pl.pallas_call(
        _out_kernel,
        grid=(B, S // TM),
        in_specs=[tok(D_MODEL), _mod_spec(), tok(gw), tok(gw), prev_spec, next_spec,
                  tok(gw), tok(gw), tok(gw), tok(gw),
                  _const_spec((D_MODEL, D_MODEL)), _const_spec((gw, gw)), _const_spec((1, gw)),
                  _const_spec((1, gw)), _const_spec((gw, gw)),
                  _const_spec((1, D_MODEL)), _const_spec((1, D_MODEL))],
        out_specs=tok(D_MODEL),
        out_shape=jax.ShapeDtypeStruct(xs.shape, F32),
        scratch_shapes=[pltpu.VMEM((TM + 2 * POOL_HALO, gw), F32)],
        compiler_params=_cparams(("parallel", "parallel")),
        name="mix_out_proj",
    )(xs, mod, a, u, u, u, hf, hb, co, dd, w_out, pool_bd, pool_scale, ml_g, gm, g, b)


def _rope_tables(L, Lc, dim):
    rows = L // GRID_W
    row = jnp.repeat(jnp.arange(rows), GRID_W).astype(F32)
    col = jnp.tile(jnp.arange(GRID_W), rows).astype(F32)
    axis_dim = dim // 2
    inv = ROPE_THETA ** (-jnp.arange(0, axis_dim, 2, dtype=F32) / axis_dim)
    ang = jnp.concatenate([row[:, None] * inv, col[:, None] * inv], axis=-1)
    cos, sin = jnp.cos(ang), jnp.sin(ang)
    reps = LANES // dim
    cos_p = jnp.tile(jnp.concatenate([cos, cos], axis=-1), (1, reps))
    sin_p = jnp.tile(jnp.concatenate([-sin, sin], axis=-1), (1, reps))
    cos_p = jnp.concatenate([jnp.ones((Lc, LANES), F32), cos_p], axis=0)
    sin_p = jnp.concatenate([jnp.zeros((Lc, LANES), F32), sin_p], axis=0)
    return cos_p, sin_p


def _reorder_w_in(w):
    g0 = 8 * GROUP_W
    g1 = g0 + 4 * ML_HEADS
    pad = jnp.zeros((w.shape[0], LANES - 4 * ML_HEADS), w.dtype)
    return jnp.concatenate([w[:, :g0], w[:, g1:], w[:, g0:g1], pad], axis=1).astype(BF16)


def _block_diag(blocks):
    n = blocks.shape[0]
    rows = []
    for i in range(n):
        rows.append(jnp.concatenate([blocks[i] if k == i else jnp.zeros_like(blocks[i])
                                     for k in range(n)], axis=1))
    return jnp.concatenate(rows, axis=0)


def kernel(x, c, ctx, c_ctx, w_ada, b_ada, ln_g, ln_b, ffn1_wi, ffn1_wo, ffn2_wi, ffn2_wo, w_in, w_out,
           diff_lambda, diff_norm_g, pool_w, pool_scale, ml_gate_b, ml_norm_g, gqa_qnorm_g, gqa_knorm_g):
    B, L, _ = x.shape
    Lc = ctx.shape[1]
    assert Lc == TM and L % KV_BLOCK == 0 and L % GRID_W == 0 and B + 1 <= MOD_ROWS
    depth = w_ada.shape[0]

    cc = jnp.concatenate([c_ctx[None], c, jnp.zeros((MOD_ROWS - 1 - B, D_MODEL), F32)], axis=0)
    mod_all = _modulation(cc, w_ada, b_ada).reshape(depth, MOD_ROWS, N_MOD, D_MODEL)

    rope = _rope_tables(L, Lc, DA_QK) + _rope_tables(L, Lc, GQA_DIM)
    gm = _block_diag(jnp.full((GROUP_W // HEAD_W, HEAD_W, HEAD_W), 1.0 / HEAD_W, F32))

    xs = jnp.concatenate([ctx, x], axis=1)
    for l in range(depth):
        mod = mod_all[l]
        lam_init = 0.8 - 0.6 * math.exp(-0.3 * l)
        xs = _ffn(xs, mod, ffn1_wi[l].astype(BF16), ffn1_wo[l].astype(BF16), ln_g[l, 0], ln_b[l, 0], 0)
        gate_b = jnp.concatenate([ml_gate_b[l].reshape(1, -1),
                                  jnp.zeros((1, LANES - 4 * ML_HEADS), F32)], axis=1)
        qg = jnp.tile(gqa_qnorm_g[l], GQA_HEADS).reshape(1, -1)
        kg = jnp.tile(gqa_knorm_g[l], GQA_KV_HEADS).reshape(1, -1)
        (qat, ka, vat, u, cq, ck, cv, co, gates, qdt, kd, vdt) = _in_proj(
            xs, mod, _reorder_w_in(w_in[l]), rope, qg, kg, gate_b, gm)
        a = _attention(functools.partial(_diff_attn_kernel, lam_init=lam_init), qat, ka, vat,
                       [diff_lambda[l], diff_norm_g[l].reshape(1, -1)], 2 * DA_HEADS, "diff_attention")
        dd = _attention(_gqa_kernel, qdt, kd, vdt, [], GQA_HEADS, "gqa_attention")
        hf, hb = _mlstm(cq, ck, cv, gates)
        xs = _out_proj(xs, mod, a, u, hf, hb, co, dd, w_out[l].astype(BF16),
                       _block_diag(pool_w[l]).astype(BF16), pool_scale[l].reshape(1, -1),
                       ml_norm_g[l].reshape(1, -1), gm, ln_g[l, 1].reshape(1, -1),
                       ln_b[l, 1].reshape(1, -1))
        xs = _ffn(xs, mod, ffn2_wi[l].astype(BF16), ffn2_wo[l].astype(BF16), ln_g[l, 2], ln_b[l, 2], 2)
    return xs[:, Lc:]
```

```python
import functools
import math

import jax
import jax.numpy as jnp
from jax import lax
from jax.experimental import pallas as pl
from jax.experimental.pallas import tpu as pltpu

F32 = jnp.float32
BF16 = jnp.bfloat16

D_MODEL = 1024
DEPTH = 2
GRID_W = 64
GROUP_W = 256
D_FF = 2816
N_MOD = 9
EPS = 1e-6
ROPE_THETA = 10000.0
DA_HEADS = 4
DA_QK = 32
DA_V = 64
POOL_WINDOWS = (2, 4, 8, 16)
POOL_GROUP = 64
POOL_HALO = 8
ML_HEADS = 4
ML_DIM = 64
GQA_HEADS = 4
GQA_KV_HEADS = 2
GQA_DIM = 64
HEAD_W = 64
LANES = 128
ALPHA = (2.0 * DEPTH) ** 0.25
LOG2E = math.log2(math.e)

TM = 256
KV_BLOCK = 512
FF_CHUNKS = ((0, 1024), (1024, 1024), (2048, 768))
IN_W = 2688
MOD_ROWS = 16
MOD_TN = 1152
VMEM_LIMIT = 56 * 1024 * 1024


def _cparams(sem):
    return pltpu.CompilerParams(dimension_semantics=sem, vmem_limit_bytes=VMEM_LIMIT)


def _const_spec(shape):
    nd = len(shape)
    return pl.BlockSpec(shape, lambda b, j: (0,) * nd, pipeline_mode=pl.Buffered(1))


def _mod_spec():
    return pl.BlockSpec((1, N_MOD, D_MODEL), lambda b, j: (jnp.where(j == 0, 0, b + 1), 0, 0))


def _layer_norm(x):
    mu = jnp.mean(x, axis=-1, keepdims=True)
    xc = x - mu
    var = jnp.mean(xc * xc, axis=-1, keepdims=True)
    return xc * lax.rsqrt(var + EPS)


def _group_mean(x, gm):
    return jnp.dot(x, gm, preferred_element_type=F32, precision=lax.Precision.HIGHEST)


def _mod_kernel(c_ref, w_ref, b_ref, o_ref):
    c = c_ref[...]
    s = c * jax.nn.sigmoid(c)
    o_ref[0] = jnp.dot(s, w_ref[0], preferred_element_type=F32,
                       precision=lax.Precision.HIGHEST) + b_ref[0]


def _modulation(cc, w_ada, b_ada):
    depth = w_ada.shape[0]
    n = N_MOD * D_MODEL
    return pl.pallas_call(
        _mod_kernel,
        grid=(depth, n // MOD_TN),
        in_specs=[pl.BlockSpec((MOD_ROWS, D_MODEL), lambda l, j: (0, 0)),
                  pl.BlockSpec((1, D_MODEL, MOD_TN), lambda l, j: (l, 0, j)),
                  pl.BlockSpec((1, 1, MOD_TN), lambda l, j: (l, 0, j))],
        out_specs=pl.BlockSpec((1, MOD_ROWS, MOD_TN), lambda l, j: (l, 0, j)),
        out_shape=jax.ShapeDtypeStruct((depth, MOD_ROWS, n), F32),
        compiler_params=_cparams(("parallel", "parallel")),
        name="adaln_mod",
    )(cc, w_ada, b_ada.reshape(depth, 1, n))


def _ffn_kernel(x_ref, mod_ref, wi_ref, wo_ref, g_ref, b_ref, o_ref, *, s):
    x = x_ref[0]
    mod = mod_ref[0]
    h = (_layer_norm(x) * (1.0 + mod[3 * s + 1:3 * s + 2]) + mod[3 * s:3 * s + 1]).astype(BF16)
    acc = jnp.zeros((x.shape[0], D_MODEL), F32)
    for c0, cw in FF_CHUNKS:
        gate = jnp.dot(h, wi_ref[:, c0:c0 + cw], preferred_element_type=F32)
        up = jnp.dot(h, wi_ref[:, D_FF + c0:D_FF + c0 + cw], preferred_element_type=F32)
        act = (gate * jax.nn.sigmoid(gate) * up).astype(BF16)
        acc = acc + jnp.dot(act, wo_ref[c0:c0 + cw, :], preferred_element_type=F32)
    y = ALPHA * x + (0.5 * mod[3 * s + 2:3 * s + 3]) * acc
    o_ref[0] = _layer_norm(y) * g_ref[...] + b_ref[...]


def _ffn(xs, mod, wi, wo, g, b, s):
    B, S, _ = xs.shape
    tile = pl.BlockSpec((1, TM, D_MODEL), lambda b_, j: (b_, j, 0))
    return pl.pallas_call(
        functools.partial(_ffn_kernel, s=s),
        grid=(B, S // TM),
        in_specs=[tile, _mod_spec(), _const_spec((D_MODEL, 2 * D_FF)), _const_spec((D_FF, D_MODEL)),
                  _const_spec((1, D_MODEL)), _const_spec((1, D_MODEL))],
        out_specs=tile,
        out_shape=jax.ShapeDtypeStruct(xs.shape, F32),
        compiler_params=_cparams(("parallel", "parallel")),
        name="macaron_ffn",
    )(xs, mod, wi, wo, g.reshape(1, D_MODEL), b.reshape(1, D_MODEL))


def _rope(x, cos, sin_signed, half):
    w = x.shape[1]
    reps = w // LANES
    if reps > 1:
        cos = jnp.concatenate([cos] * reps, axis=1)
        sin_signed = jnp.concatenate([sin_signed] * reps, axis=1)
    lane = lax.broadcasted_iota(jnp.int32, x.shape, 1)
    first = (lane % (2 * half)) < half
    partner = jnp.where(first, pltpu.roll(x, w - half, 1), pltpu.roll(x, half, 1))
    return x * cos + partner * sin_signed


def _with_ones(v, n_heads):
    ones = jnp.ones((v.shape[0], HEAD_W), v.dtype)
    parts = []
    for h in range(n_heads):
        parts += [v[:, h * HEAD_W:(h + 1) * HEAD_W], ones]
    return jnp.concatenate(parts, axis=1)


def _log_sigmoid(x):
    return jnp.minimum(x, 0.0) - jnp.log1p(jnp.exp(-jnp.abs(x)))


def _values_t_with_ones(v, n_heads):
    vt = v.T
    ones = jnp.ones((HEAD_W, v.shape[0]), v.dtype)
    parts = []
    for h in range(n_heads):
        parts += [vt[h * HEAD_W:(h + 1) * HEAD_W], ones]
    return jnp.concatenate(parts, axis=0)


def _in_kernel(x_ref, mod_ref, w_ref, cos_a, sin_a, cos_d, sin_d, qg_ref, kg_ref, gb_ref, gm_ref,
               qat_ref, ka_ref, vat_ref, u_ref, cq_ref, ck_ref, cv_ref, co_ref, gt_ref,
               qdt_ref, kd_ref, vdt_ref):
    x = x_ref[0]
    mod = mod_ref[0]
    h = (_layer_norm(x) * (1.0 + mod[4:5]) + mod[3:4]).astype(BF16)
    y = jnp.dot(h, w_ref[...], preferred_element_type=F32)
    gw = GROUP_W
    ca, sa = cos_a[...], sin_a[...]
    qa = _rope(y[:, 0:gw], ca, sa, DA_QK // 2) * (DA_QK ** -0.5 * LOG2E)
    qat_ref[0] = qa.T.astype(BF16)
    ka_ref[0] = _rope(y[:, gw:2 * gw], ca, sa, DA_QK // 2).astype(BF16)
    vat_ref[0] = _values_t_with_ones(y[:, 2 * gw:3 * gw], DA_HEADS).astype(BF16)
    u_ref[0] = y[:, 3 * gw:4 * gw]
    cq_ref[0] = (y[:, 4 * gw:5 * gw] * ML_DIM ** -0.5).astype(BF16)
    ck_ref[0] = y[:, 5 * gw:6 * gw].astype(BF16)
    cv_ref[0] = _with_ones(y[:, 6 * gw:7 * gw], ML_HEADS).astype(BF16)
    co_ref[0] = y[:, 7 * gw:8 * gw].astype(BF16)
    graw = y[:, 10 * gw:10 * gw + LANES] + gb_ref[...]
    lane = lax.broadcasted_iota(jnp.int32, graw.shape, 1)
    is_forget = ((lane // ML_HEADS) % 2) == 1
    gt_ref[0] = jnp.where(is_forget, _log_sigmoid(graw), graw)
    gm = gm_ref[...]
    cd, sd = cos_d[...], sin_d[...]
    qd = y[:, 8 * gw:9 * gw]
    qd = qd * lax.rsqrt(_group_mean(qd * qd, gm) + EPS) * qg_ref[...]
    qdt_ref[0] = (_rope(qd, cd, sd, GQA_DIM // 2) * (GQA_DIM ** -0.5 * LOG2E)).T.astype(BF16)
    kvw = GQA_KV_HEADS * GQA_DIM
    kd = y[:, 9 * gw:9 * gw + kvw]
    kd = kd * lax.rsqrt(_group_mean(kd * kd, gm[:kvw, :kvw]) + EPS) * kg_ref[...]
    kd_ref[0] = _rope(kd, cd, sd, GQA_DIM // 2).astype(BF16)
    vdt_ref[0] = _values_t_with_ones(y[:, 9 * gw + kvw:9 * gw + 2 * kvw], GQA_KV_HEADS).astype(BF16)


def _in_proj(xs, mod, w_in, rope, qg, kg, gate_b, gm):
    B, S, _ = xs.shape
    gw = GROUP_W
    kvw = GQA_KV_HEADS * GQA_DIM

    def tok(width):
        return pl.BlockSpec((1, TM, width), lambda b, j: (b, j, 0))

    def tok_t(rows):
        return pl.BlockSpec((1, rows, TM), lambda b, j: (b, 0, j))

    def table():
        return pl.BlockSpec((TM, LANES), lambda b, j: (j, 0))

    out_shapes = [
        ((B, gw, S), BF16, tok_t(gw)),
        ((B, S, gw), BF16, tok(gw)),
        ((B, 2 * gw, S), BF16, tok_t(2 * gw)),
        ((B, S, gw), F32, tok(gw)),
        ((B, S, gw), BF16, tok(gw)),
        ((B, S, gw), BF16, tok(gw)),
        ((B, S, 2 * gw), BF16, tok(2 * gw)),
        ((B, S, gw), BF16, tok(gw)),
        ((B, S, LANES), F32, tok(LANES)),
        ((B, gw, S), BF16, tok_t(gw)),
        ((B, S, kvw), BF16, tok(kvw)),
        ((B, 2 * kvw, S), BF16, tok_t(2 * kvw)),
    ]
    return pl.pallas_call(
        _in_kernel,
        grid=(B, S // TM),
        in_specs=[tok(D_MODEL), _mod_spec(), _const_spec((D_MODEL, IN_W)),
                  table(), table(), table(), table(),
                  _const_spec((1, gw)), _const_spec((1, kvw)), _const_spec((1, LANES)),
                  _const_spec((gw, gw))],
        out_specs=[o[2] for o in out_shapes],
        out_shape=[jax.ShapeDtypeStruct(o[0], o[1]) for o in out_shapes],
        compiler_params=_cparams(("parallel", "parallel")),
        name="mix_in_proj",
    )(xs, mod, w_in, rope[0], rope[1], rope[2], rope[3], qg, kg, gate_b, gm)


def _flash_absorb(st, vt, acc_ref, m_ref, c, first):
    m_blk = jnp.max(st, axis=0, keepdims=True)
    if first:
        m_new = m_blk
    else:
        m_old = m_ref[c]
        m_new = jnp.maximum(m_old, m_blk)
    p = jnp.exp2((st - m_new).astype(BF16))
    pv = jnp.dot(vt, p, preferred_element_type=F32)
    if first:
        acc_ref[c] = pv
    else:
        acc_ref[c] = jnp.exp2(m_old - m_new) * acc_ref[c] + pv
    m_ref[c] = m_new


def _flash_sweep(k_ref, vt_ref, v_row_of_chain, qw_ref, acc_ref, m_ref, st_ref):
    n_chains = len(v_row_of_chain)
    n_blk = (k_ref.shape[1] - TM) // KV_BLOCK

    def scores(k_blk, c):
        return jnp.dot(k_blk, qw_ref[c], preferred_element_type=F32)

    def vt_blk(c, start, size):
        r0 = v_row_of_chain[c]
        return vt_ref[0, r0:r0 + LANES, pl.ds(start, size)]

    def latent_start(i):
        return pl.multiple_of(TM + i * KV_BLOCK, TM)

    k_ctx = k_ref[0, 0:TM, :]
    sts = [scores(k_ctx, c) for c in range(n_chains)]
    for c in range(n_chains):
        _flash_absorb(sts[c], vt_blk(c, 0, TM), acc_ref, m_ref, c, True)

    def absorb_slot(i, slot):
        for c in range(n_chains):
            _flash_absorb(st_ref[slot, c], vt_blk(c, latent_start(i), KV_BLOCK), acc_ref, m_ref, c,
                          False)

    def half_step(i, src, dst):
        k_blk = k_ref[0, pl.ds(latent_start(i), KV_BLOCK), :]
        for c in range(n_chains):
            st_ref[dst, c] = scores(k_blk, c)
            _flash_absorb(st_ref[src, c], vt_blk(c, latent_start(i - 1), KV_BLOCK), acc_ref, m_ref,
                          c, False)

    @pl.when(pl.program_id(1) > 0)
    def _():
        k_blk = k_ref[0, pl.ds(latent_start(0), KV_BLOCK), :]
        for c in range(n_chains):
            st_ref[0, c] = scores(k_blk, c)

        def body(t, _):
            half_step(2 * t + 1, 0, 1)
            half_step(2 * t + 2, 1, 0)
            return 0

        n_pairs = (n_blk - 1) // 2
        lax.fori_loop(0, n_pairs, body, 0)
        if (n_blk - 1) % 2 == 1:
            half_step(n_blk - 1, 0, 1)
            absorb_slot(n_blk - 1, 1)
        else:
            absorb_slot(n_blk - 1, 0)


def _attn_out_t(acc):
    return acc[:HEAD_W] * (1.0 / acc[HEAD_W:HEAD_W + 1])


def _diff_attn_kernel(qt_ref, k_ref, vt_ref, lam_ref, g_ref, o_ref, qw_ref, acc_ref, m_ref, st_ref,
                      *, lam_init):
    qt = qt_ref[0]
    feat = lax.broadcasted_iota(jnp.int32, qt.shape, 0)
    n_chains = 2 * DA_HEADS
    for c in range(n_chains):
        qw_ref[c] = jnp.where(feat // DA_QK == c, qt, jnp.zeros_like(qt))
    _flash_sweep(k_ref, vt_ref, [LANES * (c // 2) for c in range(n_chains)], qw_ref, acc_ref, m_ref,
                 st_ref)
    dl = lam_ref[...]
    lam = (jnp.exp(jnp.sum(dl[0:1] * dl[1:2], axis=1, keepdims=True))
           - jnp.exp(jnp.sum(dl[2:3] * dl[3:4], axis=1, keepdims=True)) + lam_init)
    outs = []
    for h in range(DA_HEADS):
        o = _attn_out_t(acc_ref[2 * h]) - lam * _attn_out_t(acc_ref[2 * h + 1])
        outs.append(o * lax.rsqrt(jnp.mean(o * o, axis=0, keepdims=True) + EPS))
    o_nat = jnp.concatenate(outs, axis=0).T
    o_ref[0] = (o_nat * g_ref[...] * (1.0 - lam_init)).astype(BF16)


def _gqa_kernel(qt_ref, k_ref, vt_ref, o_ref, qw_ref, acc_ref, m_ref, st_ref):
    qt = qt_ref[0]
    group = GQA_HEADS // GQA_KV_HEADS
    zeros = jnp.zeros((GQA_DIM, qt.shape[1]), qt.dtype)
    for qh in range(GQA_HEADS):
        q_rows = qt[qh * GQA_DIM:(qh + 1) * GQA_DIM]
        kvh = qh // group
        qw_ref[qh] = jnp.concatenate([q_rows if i == kvh else zeros for i in range(GQA_KV_HEADS)],
                                     axis=0)
    _flash_sweep(k_ref, vt_ref, [LANES * (qh // group) for qh in range(GQA_HEADS)],
                 qw_ref, acc_ref, m_ref, st_ref)
    o_t = jnp.concatenate([_attn_out_t(acc_ref[qh]) for qh in range(GQA_HEADS)], axis=0)
    o_ref[0] = o_t.T.astype(BF16)


def _attention(kernel_fn, qt, k, vt, extra, n_chains, name):
    B, S, kw = k.shape
    gw = qt.shape[1]
    extra_specs = [_const_spec(e.shape) for e in extra]
    return pl.pallas_call(
        kernel_fn,
        grid=(B, S // TM),
        in_specs=[pl.BlockSpec((1, gw, TM), lambda b, j: (b, 0, j)),
                  pl.BlockSpec((1, S, kw), lambda b, j: (b, 0, 0)),
                  pl.BlockSpec((1,) + vt.shape[1:], lambda b, j: (b, 0, 0))] + extra_specs,
        out_specs=pl.BlockSpec((1, TM, gw), lambda b, j: (b, j, 0)),
        out_shape=jax.ShapeDtypeStruct((B, S, gw), BF16),
        scratch_shapes=[pltpu.VMEM((n_chains, kw, TM), BF16),
                        pltpu.VMEM((n_chains, LANES, TM), F32),
                        pltpu.VMEM((n_chains, 1, TM), F32),
                        pltpu.VMEM((2, n_chains, KV_BLOCK, TM), F32)],
        compiler_params=_cparams(("parallel", "arbitrary")),
        name=name,
    )(qt, k, vt, *extra)


def _mlstm_direction(q, k, vaug, g, gt16, c_ref, m_ref, d, h_ref):
    T = q.shape[0]
    row = lax.broadcasted_iota(jnp.int32, (T, T), 0)
    col = lax.broadcasted_iota(jnp.int32, (T, T), 1)
    seen = (col <= row) if d == 0 else (col >= row)
    seen_f = seen.astype(F32)
    hi = lax.Precision.HIGHEST
    b_cols = jnp.dot(seen_f, g, preferred_element_type=F32, precision=hi)
    b_rows = lax.dot_general(gt16, seen_f, (((1,), (1,)), ((), ())),
                             preferred_element_type=F32, precision=hi)
    last = T - 1 if d == 0 else 0
    outs = []
    for h in range(ML_HEADS):
        li_c = 2 * ML_HEADS * d + h
        lf_c = li_c + ML_HEADS
        b_col = b_cols[:, lf_c:lf_c + 1]
        li_col = g[:, li_c:li_c + 1]
        r_row = gt16[li_c:li_c + 1, :] - b_rows[lf_c:lf_c + 1, :]
        m_prev = m_ref[d * ML_HEADS + h:d * ML_HEADS + h + 1, 0:1]
        c_prev = c_ref[d, h]
        qh = q[:, ML_DIM * h:ML_DIM * (h + 1)]
        kh = k[:, ML_DIM * h:ML_DIM * (h + 1)]
        vh = vaug[:, LANES * h:LANES * (h + 1)]
        dmat = jnp.where(seen, b_col + r_row, -jnp.inf)
        inter = b_col + m_prev
        m_t = jnp.maximum(inter, jnp.max(dmat, axis=1, keepdims=True))
        w_intra = jnp.exp(dmat - m_t)
        w_inter = jnp.exp(inter - m_t)
        s = lax.dot_general(qh, kh, (((1,), (1,)), ((), ())), preferred_element_type=F32)
        a = (s * w_intra).astype(BF16)
        num = (jnp.dot(a, vh, preferred_element_type=F32)
               + w_inter * jnp.dot(qh, c_prev.astype(BF16), preferred_element_type=F32))
        den = jnp.maximum(jnp.abs(num[:, ML_DIM:ML_DIM + 1]), jnp.exp(-m_t))
        outs.append(num[:, :ML_DIM] / den)
        b_end = b_col[last:last + 1, :]
        g_col = b_end - b_col + li_col
        m_new = jnp.maximum(b_end + m_prev, jnp.max(g_col, axis=0, keepdims=True))
        wk = jnp.exp(g_col - m_new)
        decay = jnp.exp(b_end + m_prev - m_new)
        kw = (kh.astype(F32) * wk).astype(BF16)
        c_ref[d, h] = decay * c_prev + lax.dot_general(kw, vh, (((0,), (0,)), ((), ())),
                                                       preferred_element_type=F32)
        m_ref[d * ML_HEADS + h:d * ML_HEADS + h + 1, :] = jnp.broadcast_to(m_new, (1, LANES))
    h_ref[0] = jnp.concatenate(outs, axis=1)


def _mlstm_kernel(qf_ref, kf_ref, vf_ref, gf_ref, qb_ref, kb_ref, vb_ref, gb_ref,
                  hf_ref, hb_ref, c_ref, m_ref):
    @pl.when(pl.program_id(1) == 0)
    def _():
        c_ref[...] = jnp.zeros_like(c_ref)
        m_ref[...] = jnp.zeros_like(m_ref)

    for d, (q_ref, k_ref, v_ref, g_ref, h_ref) in enumerate(
            ((qf_ref, kf_ref, vf_ref, gf_ref, hf_ref), (qb_ref, kb_ref, vb_ref, gb_ref, hb_ref))):
        g = g_ref[0]
        gt16 = g.T[0:4 * ML_HEADS, :]
        _mlstm_direction(q_ref[0], k_ref[0], v_ref[0], g, gt16, c_ref, m_ref, d, h_ref)


def _mlstm(cq, ck, cv, gates):
    B, S, gw = cq.shape
    nc = S // TM

    def fwd(width):
        return pl.BlockSpec((1, TM, width), lambda b, i: (b, i, 0))

    def bwd(width):
        return pl.BlockSpec((1, TM, width), lambda b, i: (b, jnp.where(i == 0, 0, nc - i), 0))

    widths = (gw, gw, 2 * gw, LANES)
    return pl.pallas_call(
        _mlstm_kernel,
        grid=(B, nc),
        in_specs=[fwd(w) for w in widths] + [bwd(w) for w in widths],
        out_specs=[fwd(gw), bwd(gw)],
        out_shape=[jax.ShapeDtypeStruct((B, S, gw), F32)] * 2,
        scratch_shapes=[pltpu.VMEM((2, ML_HEADS, ML_DIM, LANES), F32),
                        pltpu.VMEM((2 * ML_HEADS, LANES), F32)],
        compiler_params=_cparams(("parallel", "arbitrary")),
        name="mlstm_scan",
    )(cq, ck, cv, gates, cq, ck, cv, gates)


def _pool_mixer(u_ref, up_ref, un_ref, ext_ref, pw_ref, ps_ref):
    j = pl.program_id(1)
    nt = pl.num_programs(1)
    has_prev = j >= 2
    has_next = jnp.logical_and(j >= 1, j < nt - 1)
    u = u_ref[0]
    ext_ref[0:POOL_HALO, :] = jnp.where(has_prev, up_ref[0], 0.0)
    ext_ref[POOL_HALO:POOL_HALO + TM, :] = u
    ext_ref[POOL_HALO + TM:, :] = jnp.where(has_next, un_ref[0], 0.0)

    def shifted(s):
        return ext_ref[POOL_HALO + s:POOL_HALO + s + TM, :]

    lane = lax.broadcasted_iota(jnp.int32, (TM, GROUP_W), 1)
    r = lax.broadcasted_iota(jnp.int32, (TM, GROUP_W), 0)
    far = 2 * POOL_HALO
    left_room = r + jnp.where(has_prev, far, 0)
    right_room = (TM - 1 - r) + jnp.where(has_next, far, 0)
    total = u
    mean = jnp.zeros_like(u)
    prev_half = 0
    for gi, w in enumerate(POOL_WINDOWS):
        half = w // 2
        for s in list(range(-half, -prev_half)) + list(range(max(prev_half, 1), half)):
            total = total + shifted(s)
        prev_half = half
        cnt = jnp.minimum(left_room, half) + jnp.minimum(right_room, half - 1) + 1
        mean = jnp.where(lane // POOL_GROUP == gi, total / cnt.astype(F32), mean)
    dlt = (mean - u).astype(BF16)
    return jnp.dot(dlt, pw_ref[...], preferred_element_type=F32) * ps_ref[...]


def _out_kernel(x_ref, mod_ref, a_ref, u_ref, up_ref, un_ref, hf_ref, hb_ref, co_ref, d_ref,
                w_ref, pw_ref, ps_ref, mg_ref, gm_ref, g_ref, b_ref, o_ref, ext_ref):
    gw = GROUP_W
    x = x_ref[0]
    mod = mod_ref[0]
    b_mix = _pool_mixer(u_ref, up_ref, un_ref, ext_ref, pw_ref, ps_ref)
    hm = hf_ref[0] + hb_ref[0]
    c_mix = (hm * lax.rsqrt(_group_mean(hm * hm, gm_ref[...]) + EPS) * mg_ref[...]
             * jax.nn.sigmoid(co_ref[0].astype(F32)))
    o = (jnp.dot(a_ref[0], w_ref[0:gw, :], preferred_element_type=F32)
         + jnp.dot(b_mix.astype(BF16), w_ref[gw:2 * gw, :], preferred_element_type=F32)
         + jnp.dot(c_mix.astype(BF16), w_ref[2 * gw:3 * gw, :], preferred_element_type=F32)
         + jnp.dot(d_ref[0], w_ref[3 * gw:4 * gw, :], preferred_element_type=F32))
    y = ALPHA * x + mod[5:6] * o
    o_ref[0] = _layer_norm(y) * g_ref[...] + b_ref[...]


def _out_proj(xs, mod, a, u, hf, hb, co, dd, w_out, pool_bd, pool_scale, ml_g, gm, g, b):
    B, S, _ = xs.shape
    gw = GROUP_W
    hb_per_tile = TM // POOL_HALO
    n_halo = S // POOL_HALO

    def tok(width):
        return pl.BlockSpec((1, TM, width), lambda b_, j: (b_, j, 0))

    prev_spec = pl.BlockSpec((1, POOL_HALO, gw),
                             lambda b_, j: (b_, jnp.maximum(j * hb_per_tile - 1, 0), 0))
    next_spec = pl.BlockSpec((1, POOL_HALO, gw),
                             lambda b_, j: (b_, jnp.minimum((j + 1) * hb_per_tile, n_halo - 1), 0))
    return pl.pallas_call(
        _out_kernel,
        grid=(B, S // TM),
        in_specs=[tok(D_MODEL), _mod_spec(), tok(gw), tok(gw), prev_spec, next_spec,
                  tok(gw), tok(gw), tok(gw), tok(gw),
                  _const_spec((D_MODEL, D_MODEL)), _const_spec((gw, gw)), _const_spec((1, gw)),
                  _const_spec((1, gw)), _const_spec((gw, gw)),
                  _const_spec((1, D_MODEL)), _const_spec((1, D_MODEL))],
        out_specs=tok(D_MODEL),
        out_shape=jax.ShapeDtypeStruct(xs.shape, F32),
        scratch_shapes=[pltpu.VMEM((TM + 2 * POOL_HALO, gw), F32)],
        compiler_params=_cparams(("parallel", "parallel")),
        name="mix_out_proj",
    )(xs, mod, a, u, u, u, hf, hb, co, dd, w_out, pool_bd, pool_scale, ml_g, gm, g, b)


def _rope_tables(L, Lc, dim):
    rows = L // GRID_W
    row = jnp.repeat(jnp.arange(rows), GRID_W).astype(F32)
    col = jnp.tile(jnp.arange(GRID_W), rows).astype(F32)
    axis_dim = dim // 2
    inv = ROPE_THETA ** (-jnp.arange(0, axis_dim, 2, dtype=F32) / axis_dim)
    ang = jnp.concatenate([row[:, None] * inv, col[:, None] * inv], axis=-1)
    cos, sin = jnp.cos(ang), jnp.sin(ang)
    reps = LANES // dim
    cos_p = jnp.tile(jnp.concatenate([cos, cos], axis=-1), (1, reps))
    sin_p = jnp.tile(jnp.concatenate([-sin, sin], axis=-1), (1, reps))
    cos_p = jnp.concatenate([jnp.ones((Lc, LANES), F32), cos_p], axis=0)
    sin_p = jnp.concatenate([jnp.zeros((Lc, LANES), F32), sin_p], axis=0)
    return cos_p, sin_p


def _reorder_w_in(w):
    g0 = 8 * GROUP_W
    g1 = g0 + 4 * ML_HEADS
    pad = jnp.zeros((w.shape[0], LANES - 4 * ML_HEADS), w.dtype)
    return jnp.concatenate([w[:, :g0], w[:, g1:], w[:, g0:g1], pad], axis=1).astype(BF16)


def _block_diag(blocks):
    n = blocks.shape[0]
    rows = []
    for i in range(n):
        rows.append(jnp.concatenate([blocks[i] if k == i else jnp.zeros_like(blocks[i])
                                     for k in range(n)], axis=1))
    return jnp.concatenate(rows, axis=0)


def kernel(x, c, ctx, c_ctx, w_ada, b_ada, ln_g, ln_b, ffn1_wi, ffn1_wo, ffn2_wi, ffn2_wo, w_in, w_out,
           diff_lambda, diff_norm_g, pool_w, pool_scale, ml_gate_b, ml_norm_g, gqa_qnorm_g, gqa_knorm_g):
    B, L, _ = x.shape
    Lc = ctx.shape[1]
    assert Lc == TM and L % KV_BLOCK == 0 and L % GRID_W == 0 and B + 1 <= MOD_ROWS
    depth = w_ada.shape[0]

    cc = jnp.concatenate([c_ctx[None], c, jnp.zeros((MOD_ROWS - 1 - B, D_MODEL), F32)], axis=0)
    mod_all = _modulation(cc, w_ada, b_ada).reshape(depth, MOD_ROWS, N_MOD, D_MODEL)

    rope = _rope_tables(L, Lc, DA_QK) + _rope_tables(L, Lc, GQA_DIM)
    gm = _block_diag(jnp.full((GROUP_W // HEAD_W, HEAD_W, HEAD_W), 1.0 / HEAD_W, F32))

    xs = jnp.concatenate([ctx, x], axis=1)
    for l in range(depth):
        mod = mod_all[l]
        lam_init = 0.8 - 0.6 * math.exp(-0.3 * l)
        xs = _ffn(xs, mod, ffn1_wi[l].astype(BF16), ffn1_wo[l].astype(BF16), ln_g[l, 0], ln_b[l, 0], 0)
        gate_b = jnp.concatenate([ml_gate_b[l].reshape(1, -1),
                                  jnp.zeros((1, LANES - 4 * ML_HEADS), F32)], axis=1)
        qg = jnp.tile(gqa_qnorm_g[l], GQA_HEADS).reshape(1, -1)
        kg = jnp.tile(gqa_knorm_g[l], GQA_KV_HEADS).reshape(1, -1)
        (qat, ka, vat, u, cq, ck, cv, co, gates, qdt, kd, vdt) = _in_proj(
            xs, mod, _reorder_w_in(w_in[l]), rope, qg, kg, gate_b, gm)
        a = _attention(functools.partial(_diff_attn_kernel, lam_init=lam_init), qat, ka, vat,
                       [diff_lambda[l], diff_norm_g[l].reshape(1, -1)], 2 * DA_HEADS, "diff_attention")
        dd = _attention(_gqa_kernel, qdt, kd, vdt, [], GQA_HEADS, "gqa_attention")
        hf, hb = _mlstm(cq, ck, cv, gates)
        xs = _out_proj(xs, mod, a, u, hf, hb, co, dd, w_out[l].astype(BF16),
                       _block_diag(pool_w[l]).astype(BF16), pool_scale[l].reshape(1, -1),
                       ml_norm_g[l].reshape(1, -1), gm, ln_g[l, 1].reshape(1, -1),
                       ln_b[l, 1].reshape(1, -1))
        xs = _ffn(xs, mod, ffn2_wi[l].astype(BF16), ffn2_wo[l].astype(BF16), ln_g[l, 2], ln_b[l, 2], 2)
    return xs[:, Lc:]
```

```python
import functools
import math

import jax
import jax.numpy as jnp
from jax import lax
from jax.experimental import pallas as pl
from jax.experimental.pallas import tpu as pltpu

F32 = jnp.float32
BF16 = jnp.bfloat16

D_MODEL = 1024
DEPTH = 2
GRID_W = 64
GROUP_W = 256
D_FF = 2816
N_MOD = 9
EPS = 1e-6
ROPE_THETA = 10000.0
DA_HEADS = 4
DA_QK = 32
DA_V = 64
POOL_WINDOWS = (2, 4, 8, 16)
POOL_GROUP = 64
POOL_HALO = 8
ML_HEADS = 4
ML_DIM = 64
ML_GATE_COLS = 4 * ML_HEADS
GQA_HEADS = 4
GQA_KV_HEADS = 2
GQA_DIM = 64
HEAD_W = 64
LANES = 128
BF16_SUBLANES = 16
VT_ROWS = HEAD_W + BF16_SUBLANES
ALPHA = (2.0 * DEPTH) ** 0.25
LOG2E = math.log2(math.e)

TM = 256
KV_BLOCK = 512
FF_CHUNKS = ((0, 1024), (1024, 1024), (2048, 768))
IN_W = 2688
MOD_ROWS = 16
MOD_TN = 1152
VMEM_LIMIT = 56 * 1024 * 1024


def _cparams(sem):
    return pltpu.CompilerParams(dimension_semantics=sem, vmem_limit_bytes=VMEM_LIMIT)


def _const_spec(shape):
    nd = len(shape)
    return pl.BlockSpec(shape, lambda b, j: (0,) * nd, pipeline_mode=pl.Buffered(1))


def _mod_spec():
    return pl.BlockSpec((1, N_MOD, D_MODEL), lambda b, j: (jnp.where(j == 0, 0, b + 1), 0, 0))


def _layer_norm(x):
    mu = jnp.mean(x, axis=-1, keepdims=True)
    xc = x - mu
    var = jnp.mean(xc * xc, axis=-1, keepdims=True)
    return xc * lax.rsqrt(var + EPS)


def _group_mean(x, gm):
    hi = x.astype(BF16)
    lo = (x - hi.astype(F32)).astype(BF16)
    return (jnp.dot(hi, gm, preferred_element_type=F32) + jnp.dot(lo, gm, preferred_element_type=F32))


def _mod_kernel(c_ref, w_ref, b_ref, o_ref):
    c = c_ref[...]
    s = c * jax.nn.sigmoid(c)
    o_ref[0] = jnp.dot(s, w_ref[0], preferred_element_type=F32,
                       precision=lax.Precision.HIGHEST) + b_ref[0]


def _modulation(cc, w_ada, b_ada):
    depth = w_ada.shape[0]
    n = N_MOD * D_MODEL
    return pl.pallas_call(
        _mod_kernel,
        grid=(depth, n // MOD_TN),
        in_specs=[pl.BlockSpec((MOD_ROWS, D_MODEL), lambda l, j: (0, 0)),
                  pl.BlockSpec((1, D_MODEL, MOD_TN), lambda l, j: (l, 0, j)),
                  pl.BlockSpec((1, 1, MOD_TN), lambda l, j: (l, 0, j))],
        out_specs=pl.BlockSpec((1, MOD_ROWS, MOD_TN), lambda l, j: (l, 0, j)),
        out_shape=jax.ShapeDtypeStruct((depth, MOD_ROWS, n), F32),
        compiler_params=_cparams(("parallel", "parallel")),
        name="adaln_mod",
    )(cc, w_ada, b_ada.reshape(depth, 1, n))


def _ffn_kernel(x_ref, mod_ref, wi_ref, wo_ref, g_ref, b_ref, o_ref, *, s):
    x = x_ref[0]
    mod = mod_ref[0]
    h = (_layer_norm(x) * (1.0 + mod[3 * s + 1:3 * s + 2]) + mod[3 * s:3 * s + 1]).astype(BF16)
    acc = jnp.zeros((x.shape[0], D_MODEL), F32)
    for c0, cw in FF_CHUNKS:
        gate = jnp.dot(h, wi_ref[:, c0:c0 + cw], preferred_element_type=F32)
        up = jnp.dot(h, wi_ref[:, D_FF + c0:D_FF + c0 + cw], preferred_element_type=F32)
        act = (gate * jax.nn.sigmoid(gate) * up).astype(BF16)
        acc = acc + jnp.dot(act, wo_ref[c0:c0 + cw, :], preferred_element_type=F32)
    y = ALPHA * x + (0.5 * mod[3 * s + 2:3 * s + 3]) * acc
    o_ref[0] = _layer_norm(y) * g_ref[...] + b_ref[...]


def _ffn(xs, mod, wi, wo, g, b, s):
    B, S, _ = xs.shape
    tile = pl.BlockSpec((1, TM, D_MODEL), lambda b_, j: (b_, j, 0))
    return pl.pallas_call(
        functools.partial(_ffn_kernel, s=s),
        grid=(B, S // TM),
        in_specs=[tile, _mod_spec(), _const_spec((D_MODEL, 2 * D_FF)), _const_spec((D_FF, D_MODEL)),
                  _const_spec((1, D_MODEL)), _const_spec((1, D_MODEL))],
        out_specs=tile,
        out_shape=jax.ShapeDtypeStruct(xs.shape, F32),
        compiler_params=_cparams(("parallel", "parallel")),
        name="macaron_ffn",
    )(xs, mod, wi, wo, g.reshape(1, D_MODEL), b.reshape(1, D_MODEL))


def _rope(x, cos, sin_signed, half):
    w = x.shape[1]
    reps = w // LANES
    if reps > 1:
        cos = jnp.concatenate([cos] * reps, axis=1)
        sin_signed = jnp.concatenate([sin_signed] * reps, axis=1)
    lane = lax.broadcasted_iota(jnp.int32, x.shape, 1)
    first = (lane % (2 * half)) < half
    partner = jnp.where(first, pltpu.roll(x, w - half, 1), pltpu.roll(x, half, 1))
    return x * cos + partner * sin_signed


def _log_sigmoid(x):
    return jnp.minimum(x, 0.0) - jnp.log1p(jnp.exp(-jnp.abs(x)))


def _values_t_with_ones(v, n_heads):
    vt = v.T
    ones = jnp.ones((VT_ROWS - HEAD_W, v.shape[0]), v.dtype)
    parts = []
    for h in range(n_heads):
        parts += [vt[h * HEAD_W:(h + 1) * HEAD_W], ones]
    return jnp.concatenate(parts, axis=0)


def _mlstm_gate_slab(graw):
    t = graw.shape[0]
    lane = lax.broadcasted_iota(jnp.int32, graw.shape, 1)
    is_forget = ((lane // ML_HEADS) % 2) == 1
    gl = jnp.where(is_forget, _log_sigmoid(graw), graw)
    row = lax.broadcasted_iota(jnp.int32, (t, t), 0)
    col = lax.broadcasted_iota(jnp.int32, (t, t), 1)
    tri = jnp.where(col <= row, 1.0, 0.0).astype(BF16)
    hi = gl.astype(BF16)
    r1 = gl - hi.astype(F32)
    mid = r1.astype(BF16)
    lo = (r1 - mid.astype(F32)).astype(BF16)
    csum = (jnp.dot(tri, hi, preferred_element_type=F32) + jnp.dot(tri, mid, preferred_element_type=F32)
            + jnp.dot(tri, lo, preferred_element_type=F32))
    rsum = csum[t - 1:t] - csum + gl
    cum = jnp.where(lane >= 2 * ML_HEADS, rsum, csum)
    return jnp.where(lane < ML_GATE_COLS, gl, pltpu.roll(cum, ML_GATE_COLS, 1))


def _in_kernel(x_ref, mod_ref, w_ref, cos_a, sin_a, cos_d, sin_d, qg_ref, kg_ref, gb_ref, gm_ref,
               qat_ref, ka_ref, vat_ref, u_ref, cqt_ref, ck_ref, cvt_ref, co_ref, g_ref, gt_ref,
               qdt_ref, kd_ref, vdt_ref):
    x = x_ref[0]
    mod = mod_ref[0]
    h = (_layer_norm(x) * (1.0 + mod[4:5]) + mod[3:4]).astype(BF16)
    y = jnp.dot(h, w_ref[...], preferred_element_type=F32)
    gw = GROUP_W
    ca, sa = cos_a[...], sin_a[...]
    qa = _rope(y[:, 0:gw], ca, sa, DA_QK // 2) * (DA_QK ** -0.5 * LOG2E)
    qat_ref[0] = qa.T.astype(BF16)
    ka_ref[0] = _rope(y[:, gw:2 * gw], ca, sa, DA_QK // 2).astype(BF16)
    vat_ref[0] = _values_t_with_ones(y[:, 2 * gw:3 * gw], DA_HEADS).astype(BF16)
    u_ref[0] = y[:, 3 * gw:4 * gw]
    cqt_ref[0] = (y[:, 4 * gw:5 * gw] * ML_DIM ** -0.5).T.astype(BF16)
    ck_ref[0] = y[:, 5 * gw:6 * gw].astype(BF16)
    cvt_ref[0] = _values_t_with_ones(y[:, 6 * gw:7 * gw], ML_HEADS).astype(BF16)
    co_ref[0] = y[:, 7 * gw:8 * gw].astype(BF16)
    gates = _mlstm_gate_slab(y[:, 10 * gw:10 * gw + LANES] + gb_ref[...])
    g_ref[0] = gates
    gt_ref[0] = gates.T[0:2 * ML_GATE_COLS]
    gm = gm_ref[...]
    cd, sd = cos_d[...], sin_d[...]
    qd = y[:, 8 * gw:9 * gw]
    qd = qd * lax.rsqrt(_group_mean(qd * qd, gm) + EPS) * qg_ref[...]
    qdt_ref[0] = (_rope(qd, cd, sd, GQA_DIM // 2) * (GQA_DIM ** -0.5 * LOG2E)).T.astype(BF16)
    kvw = GQA_KV_HEADS * GQA_DIM
    kd = y[:, 9 * gw:9 * gw + kvw]
    kd = kd * lax.rsqrt(_group_mean(kd * kd, gm[:kvw, :kvw]) + EPS) * kg_ref[...]
    kd_ref[0] = _rope(kd, cd, sd, GQA_DIM // 2).astype(BF16)
    vdt_ref[0] = _values_t_with_ones(y[:, 9 * gw + kvw:9 * gw + 2 * kvw], GQA_KV_HEADS).astype(BF16)


def _in_proj(xs, mod, w_in, rope, qg, kg, gate_b, gm):
    B, S, _ = xs.shape
    gw = GROUP_W
    kvw = GQA_KV_HEADS * GQA_DIM

    def tok(width):
        return pl.BlockSpec((1, TM, width), lambda b, j: (b, j, 0))

    def tok_t(rows):
        return pl.BlockSpec((1, rows, TM), lambda b, j: (b, 0, j))

    def table():
        return pl.BlockSpec((TM, LANES), lambda b, j: (j, 0))

    out_shapes = [
        ((B, gw, S), BF16, tok_t(gw)),
        ((B, S, gw), BF16, tok(gw)),
        ((B, DA_HEADS * VT_ROWS, S), BF16, tok_t(DA_HEADS * VT_ROWS)),
        ((B, S, gw), F32, tok(gw)),
        ((B, gw, S), BF16, tok_t(gw)),
        ((B, S, gw), BF16, tok(gw)),
        ((B, ML_HEADS * VT_ROWS, S), BF16, tok_t(ML_HEADS * VT_ROWS)),
        ((B, S, gw), BF16, tok(gw)),
        ((B, S, LANES), F32, tok(LANES)),
        ((B, 2 * ML_GATE_COLS, S), F32, tok_t(2 * ML_GATE_COLS)),
        ((B, gw, S), BF16, tok_t(gw)),
        ((B, S, kvw), BF16, tok(kvw)),
        ((B, GQA_KV_HEADS * VT_ROWS, S), BF16, tok_t(GQA_KV_HEADS * VT_ROWS)),
    ]
    return pl.pallas_call(
        _in_kernel,
        grid=(B, S // TM),
        in_specs=[tok(D_MODEL), _mod_spec(), _const_spec((D_MODEL, IN_W)),
                  table(), table(), table(), table(),
                  _const_spec((1, gw)), _const_spec((1, kvw)), _const_spec((1, LANES)),
                  _const_spec((gw, gw))],
        out_specs=[o[2] for o in out_shapes],
        out_shape=[jax.ShapeDtypeStruct(o[0], o[1]) for o in out_shapes],
        compiler_params=_cparams(("parallel", "parallel")),
        name="mix_in_proj",
    )(xs, mod, w_in, rope[0], rope[1], rope[2], rope[3], qg, kg, gate_b, gm)


def _flash_absorb(st, vt, acc_ref, m_ref, c, first):
    m_blk = jnp.max(st, axis=0, keepdims=True)
    if first:
        m_new = m_blk
    else:
        m_old = m_ref[c]
        m_new = jnp.maximum(m_old, m_blk)
    p = jnp.exp2((st - m_new).astype(BF16))
    pv = jnp.dot(vt, p, preferred_element_type=F32)
    if first:
        acc_ref[c] = pv
    else:
        acc_ref[c] = jnp.exp2(m_old - m_new) * acc_ref[c] + pv
    m_ref[c] = m_new


def _flash_sweep(k_ref, vt_ref, v_row_of_chain, qw_ref, acc_ref, m_ref, st_ref):
    n_chains = len(v_row_of_chain)
    n_blk = (k_ref.shape[1] - TM) // KV_BLOCK

    def scores(k_blk, c):
        return jnp.dot(k_blk, qw_ref[c], preferred_element_type=F32)

    def vt_blk(c, start, size):
        r0 = v_row_of_chain[c]
        return vt_ref[0, r0:r0 + VT_ROWS, pl.ds(start, size)]

    def latent_start(i):
        return pl.multiple_of(TM + i * KV_BLOCK, TM)

    k_ctx = k_ref[0, 0:TM, :]
    sts = [scores(k_ctx, c) for c in range(n_chains)]
    for c in range(n_chains):
        _flash_absorb(sts[c], vt_blk(c, 0, TM), acc_ref, m_ref, c, True)

    def absorb_slot(i, slot):
        for c in range(n_chains):
            _flash_absorb(st_ref[slot, c], vt_blk(c, latent_start(i), KV_BLOCK), acc_ref, m_ref, c,
                          False)

    def half_step(i, src, dst):
        k_blk = k_ref[0, pl.ds(latent_start(i), KV_BLOCK), :]
        for c in range(n_chains):
            st_ref[dst, c] = scores(k_blk, c)
            _flash_absorb(st_ref[src, c], vt_blk(c, latent_start(i - 1), KV_BLOCK), acc_ref, m_ref,
                          c, False)

    @pl.when(pl.program_id(1) > 0)
    def _():
        k_blk = k_ref[0, pl.ds(latent_start(0), KV_BLOCK), :]
        for c in range(n_chains):
            st_ref[0, c] = scores(k_blk, c)

        def body(t, _):
            half_step(2 * t + 1, 0, 1)
            half_step(2 * t + 2, 1, 0)
            return 0

        n_pairs = (n_blk - 1) // 2
        lax.fori_loop(0, n_pairs, body, 0)
        if (n_blk - 1) % 2 == 1:
            half_step(n_blk - 1, 0, 1)
            absorb_slot(n_blk - 1, 1)
        else:
            absorb_slot(n_blk - 1, 0)


def _attn_out_t(acc):
    return acc[:HEAD_W] * (1.0 / acc[HEAD_W:HEAD_W + 1])


def _diff_attn_kernel(qt_ref, k_ref, vt_ref, lam_ref, g_ref, o_ref, qw_ref, acc_ref, m_ref, st_ref,
                      *, lam_init):
    qt = qt_ref[0]
    feat = lax.broadcasted_iota(jnp.int32, qt.shape, 0)
    n_chains = 2 * DA_HEADS
    for c in range(n_chains):
        qw_ref[c] = jnp.where(feat // DA_QK == c, qt, jnp.zeros_like(qt))
    _flash_sweep(k_ref, vt_ref, [VT_ROWS * (c // 2) for c in range(n_chains)], qw_ref, acc_ref, m_ref,
                 st_ref)
    dl = lam_ref[...]
    lam = (jnp.exp(jnp.sum(dl[0:1] * dl[1:2], axis=1, keepdims=True))
           - jnp.exp(jnp.sum(dl[2:3] * dl[3:4], axis=1, keepdims=True)) + lam_init)
    outs = []
    for h in range(DA_HEADS):
        o = _attn_out_t(acc_ref[2 * h]) - lam * _attn_out_t(acc_ref[2 * h + 1])
        outs.append(o * lax.rsqrt(jnp.mean(o * o, axis=0, keepdims=True) + EPS))
    o_nat = jnp.concatenate(outs, axis=0).T
    o_ref[0] = (o_nat * g_ref[...] * (1.0 - lam_init)).astype(BF16)


def _gqa_kernel(qt_ref, k_ref, vt_ref, o_ref, qw_ref, acc_ref, m_ref, st_ref):
    qt = qt_ref[0]
    group = GQA_HEADS // GQA_KV_HEADS
    zeros = jnp.zeros((GQA_DIM, qt.shape[1]), qt.dtype)
    for qh in range(GQA_HEADS):
        q_rows = qt[qh * GQA_DIM:(qh + 1) * GQA_DIM]
        kvh = qh // group
        qw_ref[qh] = jnp.concatenate([q_rows if i == kvh else zeros for i in range(GQA_KV_HEADS)],
                                     axis=0)
    _flash_sweep(k_ref, vt_ref, [VT_ROWS * (qh // group) for qh in range(GQA_HEADS)],
                 qw_ref, acc_ref, m_ref, st_ref)
    o_t = jnp.concatenate([_attn_out_t(acc_ref[qh]) for qh in range(GQA_HEADS)], axis=0)
    o_ref[0] = o_t.T.astype(BF16)


def _attention(kernel_fn, qt, k, vt, extra, n_chains, name):
    B, S, kw = k.shape
    gw = qt.shape[1]
    extra_specs = [_const_spec(e.shape) for e in extra]
    return pl.pallas_call(
        kernel_fn,
        grid=(B, S // TM),
        in_specs=[pl.BlockSpec((1, gw, TM), lambda b, j: (b, 0, j)),
                  pl.BlockSpec((1, S, kw), lambda b, j: (b, 0, 0)),
                  pl.BlockSpec((1,) + vt.shape[1:], lambda b, j: (b, 0, 0))] + extra_specs,
        out_specs=pl.BlockSpec((1, TM, gw), lambda b, j: (b, j, 0)),
        out_shape=jax.ShapeDtypeStruct((B, S, gw), BF16),
        scratch_shapes=[pltpu.VMEM((n_chains, kw, TM), BF16),
                        pltpu.VMEM((n_chains, VT_ROWS, TM), F32),
                        pltpu.VMEM((n_chains, 1, TM), F32),
                        pltpu.VMEM((2, n_chains, KV_BLOCK, TM), F32)],
        compiler_params=_cparams(("parallel", "arbitrary")),
        name=name,
    )(qt, k, vt, *extra)


def _mlstm_direction(k, qt, vt, g, gt, c_ref, m_ref, d, ht_ref):
    T = k.shape[0]
    key_pos = lax.broadcasted_iota(jnp.int32, (T, T), 0)
    qry_pos = lax.broadcasted_iota(jnp.int32, (T, T), 1)
    seen = (key_pos <= qry_pos) if d == 0 else (key_pos >= qry_pos)
    feat = lax.broadcasted_iota(jnp.int32, qt.shape, 0)
    last = T - 1 if d == 0 else 0
    outs = []
    for h in range(ML_HEADS):
        li_c = 2 * ML_HEADS * d + h
        b_c = ML_GATE_COLS + li_c + ML_HEADS
        sidx = d * ML_HEADS + h
        qtm = jnp.where(feat // ML_DIM == h, qt, jnp.zeros_like(qt))
        r_col = g[:, li_c:li_c + 1] - g[:, b_c:b_c + 1]
        b_row = gt[b_c:b_c + 1, :]
        li_row = gt[li_c:li_c + 1, :]
        m_prev = m_ref[sidx:sidx + 1, 0:1]
        c_prev = c_ref[d, h]
        vth = vt[VT_ROWS * h:VT_ROWS * (h + 1)]
        st = jnp.dot(k, qtm, preferred_element_type=F32)
        dmat = jnp.where(seen, b_row + r_col, -jnp.inf)
        inter = b_row + m_prev
        m_t = jnp.maximum(inter, jnp.max(dmat, axis=0, keepdims=True))
        a = (st * jnp.exp(dmat - m_t)).astype(BF16)
        w_inter = jnp.exp(inter - m_t)
        num = (jnp.dot(vth, a, preferred_element_type=F32)
               + w_inter * jnp.dot(c_prev.astype(BF16), qtm, preferred_element_type=F32))
        den = jnp.maximum(jnp.abs(num[ML_DIM:ML_DIM + 1]), jnp.exp(-m_t))
        outs.append(num[:ML_DIM] * (1.0 / den))
        b_end = b_row[:, last:last + 1]
        g_row = b_end - b_row + li_row
        m_new = jnp.maximum(b_end + m_prev, jnp.max(g_row, axis=1, keepdims=True))
        wk = jnp.exp(g_row - m_new)
        decay = jnp.exp(b_end + m_prev - m_new)
        vw = (vth.astype(F32) * wk).astype(BF16)
        c_ref[d, h] = decay * c_prev + jnp.dot(vw, k, preferred_element_type=F32)
        m_ref[sidx:sidx + 1, :] = jnp.broadcast_to(m_new, (1, LANES))
    ht_ref[0] = jnp.concatenate(outs, axis=0)


def _mlstm_kernel(kf_ref, qf_ref, vf_ref, gf_ref, gtf_ref, kb_ref, qb_ref, vb_ref, gb_ref, gtb_ref,
                  hf_ref, hb_ref, c_ref, m_ref):
    @pl.when(pl.program_id(1) == 0)
    def _():
        c_ref[...] = jnp.zeros_like(c_ref)
        m_ref[...] = jnp.zeros_like(m_ref)

    _mlstm_direction(kf_ref[0], qf_ref[0], vf_ref[0], gf_ref[0], gtf_ref[0], c_ref, m_ref, 0, hf_ref)
    _mlstm_direction(kb_ref[0], qb_ref[0], vb_ref[0], gb_ref[0], gtb_ref[0], c_ref, m_ref, 1, hb_ref)


def _mlstm(ck, cqt, cvt, gates, gates_t):
    B, S, gw = ck.shape
    nc = S // TM

    def fwd_chunk(i):
        return i

    def bwd_chunk(i):
        return jnp.where(i == 0, 0, nc - i)

    def specs(chunk):
        def tok(width):
            return pl.BlockSpec((1, TM, width), lambda b, i: (b, chunk(i), 0))

        def tok_t(rows):
            return pl.BlockSpec((1, rows, TM), lambda b, i: (b, 0, chunk(i)))

        return [tok(gw), tok_t(gw), tok_t(cvt.shape[1]), tok(LANES), tok_t(gates_t.shape[1])], tok_t(gw)

    in_f, out_f = specs(fwd_chunk)
    in_b, out_b = specs(bwd_chunk)
    return pl.pallas_call(
        _mlstm_kernel,
        grid=(B, nc),
        in_specs=in_f + in_b,
        out_specs=[out_f, out_b],
        out_shape=[jax.ShapeDtypeStruct((B, gw, S), F32)] * 2,
        scratch_shapes=[pltpu.VMEM((2, ML_HEADS, VT_ROWS, gw), F32),
                        pltpu.VMEM((2 * ML_HEADS, LANES), F32)],
        compiler_params=_cparams(("parallel", "arbitrary")),
        name="mlstm_scan",
    )(ck, cqt, cvt, gates, gates_t, ck, cqt, cvt, gates, gates_t)


def _pool_mixer(u_ref, up_ref, un_ref, ext_ref, pw_ref, ps_ref):
    j = pl.program_id(1)
    nt = pl.num_programs(1)
    has_prev = j >= 2
    has_next = jnp.logical_and(j >= 1, j < nt - 1)
    u = u_ref[0]
    ext_ref[0:POOL_HALO, :] = jnp.where(has_prev, up_ref[0], 0.0)
    ext_ref[POOL_HALO:POOL_HALO + TM, :] = u
    ext_ref[POOL_HALO + TM:, :] = jnp.where(has_next, un_ref[0], 0.0)

    def shifted(s):
        return ext_ref[POOL_HALO + s:POOL_HALO + s + TM, :]

    lane = lax.broadcasted_iota(jnp.int32, (TM, GROUP_W), 1)
    r = lax.broadcasted_iota(jnp.int32, (TM, GROUP_W), 0)
    far = 2 * POOL_HALO
    left_room = r + jnp.where(has_prev, far, 0)
    right_room = (TM - 1 - r) + jnp.where(has_next, far, 0)
    total = u
    mean = jnp.zeros_like(u)
    prev_half = 0
    for gi, w in enumerate(POOL_WINDOWS):
        half = w // 2
        for s in list(range(-half, -prev_half)) + list(range(max(prev_half, 1), half)):
            total = total + shifted(s)
        prev_half = half
        cnt = jnp.minimum(left_room, half) + jnp.minimum(right_room, half - 1) + 1
        mean = jnp.where(lane // POOL_GROUP == gi, total / cnt.astype(F32), mean)
    dlt = (mean - u).astype(BF16)
    return jnp.dot(dlt, pw_ref[...], preferred_element_type=F32) * ps_ref[...]


def _out_kernel(x_ref, mod_ref, a_ref, u_ref, up_ref, un_ref, hf_ref, hb_ref, co_ref, d_ref,
                w_ref, pw_ref, ps_ref, mg_ref, gm_ref, g_ref, b_ref, o_ref, ext_ref):
    gw = GROUP_W
    x = x_ref[0]
    mod = mod_ref[0]
    b_mix = _pool_mixer(u_ref, up_ref, un_ref, ext_ref, pw_ref, ps_ref)
    hm = (hf_ref[0] + hb_ref[0]).T
    c_mix = (hm * lax.rsqrt(_group_mean(hm * hm, gm_ref[...]) + EPS) * mg_ref[...]
             * jax.nn.sigmoid(co_ref[0].astype(F32)))
    o = (jnp.dot(a_ref[0], w_ref[0:gw, :], preferred_element_type=F32)
         + jnp.dot(b_mix.astype(BF16), w_ref[gw:2 * gw, :], preferred_element_type=F32)
         + jnp.dot(c_mix.astype(BF16), w_ref[2 * gw:3 * gw, :], preferred_element_type=F32)
         + jnp.dot(d_ref[0], w_ref[3 * gw:4 * gw, :], preferred_element_type=F32))
    y = ALPHA * x + mod[5:6] * o
    o_ref[0] = _layer_norm(y) * g_ref[...] + b_ref[...]


def _out_proj(xs, mod, a, u, hf, hb, co, dd, w_out, pool_bd, pool_scale, ml_g, gm, g, b):
    B, S, _ = xs.shape
    gw = GROUP_W
    hb_per_tile = TM // POOL_HALO
    n_halo = S // POOL_HALO

    def tok(width):
        return pl.BlockSpec((1, TM, width), lambda b_, j: (b_, j, 0))

    def tok_t(rows):
        return pl.BlockSpec((1, rows, TM), lambda b_, j: (b_, 0, j))

    prev_spec = pl.BlockSpec((1, POOL_HALO, gw),
                             lambda b_, j: (b_, jnp.maximum(j * hb_per_tile - 1, 0), 0))
    next_spec = pl.BlockSpec((1, POOL_HALO, gw),
                             lambda b_, j: (b_, jnp.minimum((j + 1) * hb_per_tile, n_halo - 1), 0))
    return pl.pallas_call(
        _out_kernel,
        grid=(B, S // TM),
        in_specs=[tok(D_MODEL), _mod_spec(), tok(gw), tok(gw), prev_spec, next_spec,
                  tok_t(gw), tok_t(gw), tok(gw), tok(gw),
                  _const_spec((D_MODEL, D_MODEL)), _const_spec((gw, gw)), _const_spec((1, gw)),
                  _const_spec((1, gw)), _const_spec((gw, gw)),
                  _const_spec((1, D_MODEL)), _const_spec((1, D_MODEL))],
        out_specs=tok(D_MODEL),
        out_shape=jax.ShapeDtypeStruct(xs.shape, F32),
        scratch_shapes=[pltpu.VMEM((TM + 2 * POOL_HALO, gw), F32)],
        compiler_params=_cparams(("parallel", "parallel")),
        name="mix_out_proj",
    )(xs, mod, a, u, u, u, hf, hb, co, dd, w_out, pool_bd, pool_scale, ml_g, gm, g, b)


def _rope_tables(L, Lc, dim):
    rows = L // GRID_W
    row = jnp.repeat(jnp.arange(rows), GRID_W).astype(F32)
    col = jnp.tile(jnp.arange(GRID_W), rows).astype(F32)
    axis_dim = dim // 2
    inv = ROPE_THETA ** (-jnp.arange(0, axis_dim, 2, dtype=F32) / axis_dim)
    ang = jnp.concatenate([row[:, None] * inv, col[:, None] * inv], axis=-1)
    cos, sin = jnp.cos(ang), jnp.sin(ang)
    reps = LANES // dim
    cos_p = jnp.tile(jnp.concatenate([cos, cos], axis=-1), (1, reps))
    sin_p = jnp.tile(jnp.concatenate([-sin, sin], axis=-1), (1, reps))
    cos_p = jnp.concatenate([jnp.ones((Lc, LANES), F32), cos_p], axis=0)
    sin_p = jnp.concatenate([jnp.zeros((Lc, LANES), F32), sin_p], axis=0)
    return cos_p, sin_p


def _reorder_w_in(w):
    g0 = 8 * GROUP_W
    g1 = g0 + 4 * ML_HEADS
    pad = jnp.zeros((w.shape[0], LANES - 4 * ML_HEADS), w.dtype)
    return jnp.concatenate([w[:, :g0], w[:, g1:], w[:, g0:g1], pad], axis=1).astype(BF16)


def _block_diag(blocks):
    n = blocks.shape[0]
    rows = []
    for i in range(n):
        rows.append(jnp.concatenate([blocks[i] if k == i else jnp.zeros_like(blocks[i])
                                     for k in range(n)], axis=1))
    return jnp.concatenate(rows, axis=0)


def kernel(x, c, ctx, c_ctx, w_ada, b_ada, ln_g, ln_b, ffn1_wi, ffn1_wo, ffn2_wi, ffn2_wo, w_in, w_out,
           diff_lambda, diff_norm_g, pool_w, pool_scale, ml_gate_b, ml_norm_g, gqa_qnorm_g, gqa_knorm_g):
    B, L, _ = x.shape
    Lc = ctx.shape[1]
    assert Lc == TM and L % KV_BLOCK == 0 and L % GRID_W == 0 and B + 1 <= MOD_ROWS
    depth = w_ada.shape[0]

    cc = jnp.concatenate([c_ctx[None], c, jnp.zeros((MOD_ROWS - 1 - B, D_MODEL), F32)], axis=0)
    mod_all = _modulation(cc, w_ada, b_ada).reshape(depth, MOD_ROWS, N_MOD, D_MODEL)

    rope = _rope_tables(L, Lc, DA_QK) + _rope_tables(L, Lc, GQA_DIM)
    gm = _block_diag(jnp.full((GROUP_W // HEAD_W, HEAD_W, HEAD_W), 1.0 / HEAD_W, BF16))

    xs = jnp.concatenate([ctx, x], axis=1)
    for l in range(depth):
        mod = mod_all[l]
        lam_init = 0.8 - 0.6 * math.exp(-0.3 * l)
        xs = _ffn(xs, mod, ffn1_wi[l].astype(BF16), ffn1_wo[l].astype(BF16), ln_g[l, 0], ln_b[l, 0], 0)
        gate_b = jnp.concatenate([ml_gate_b[l].reshape(1, -1),
                                  jnp.zeros((1, LANES - 4 * ML_HEADS), F32)], axis=1)
        qg = jnp.tile(gqa_qnorm_g[l], GQA_HEADS).reshape(1, -1)
        kg = jnp.tile(gqa_knorm_g[l], GQA_KV_HEADS).reshape(1, -1)
        (qat, ka, vat, u, cqt, ck, cvt, co, gates, gates_t, qdt, kd, vdt) = _in_proj(
            xs, mod, _reorder_w_in(w_in[l]), rope, qg, kg, gate_b, gm)
        a = _attention(functools.partial(_diff_attn_kernel, lam_init=lam_init), qat, ka, vat,
                       [diff_lambda[l], diff_norm_g[l].reshape(1, -1)], 2 * DA_HEADS, "diff_attention")
        dd = _attention(_gqa_kernel, qdt, kd, vdt, [], GQA_HEADS, "gqa_attention")
        hf, hb = _mlstm(ck, cqt, cvt, gates, gates_t)
        xs = _out_proj(xs, mod, a, u, hf, hb, co, dd, w_out[l].astype(BF16),
                       _block_diag(pool_w[l]).astype(BF16), pool_scale[l].reshape(1, -1),
                       ml_norm_g[l].reshape(1, -1), gm, ln_g[l, 1].reshape(1, -1),
                       ln_b[l, 1].reshape(1, -1))
        xs = _ffn(xs, mod, ffn2_wi[l].astype(BF16), ffn2_wo[l].astype(BF16), ln_g[l, 2], ln_b[l, 2], 2)
    return xs[:, Lc:]
```

```python
import functools
import math

import jax
import jax.numpy as jnp
from jax import lax
from jax.experimental import pallas as pl
from jax.experimental.pallas import tpu as pltpu

F32 = jnp.float32
BF16 = jnp.bfloat16

D_MODEL = 1024
DEPTH = 2
GRID_W = 64
GROUP_W = 256
D_FF = 2816
N_MOD = 9
EPS = 1e-6
ROPE_THETA = 10000.0
DA_HEADS = 4
DA_QK = 32
DA_V = 64
POOL_WINDOWS = (2, 4, 8, 16)
POOL_GROUP = 64
POOL_HALO = 8
ML_HEADS = 4
ML_DIM = 64
ML_GATE_COLS = 4 * ML_HEADS
GQA_HEADS = 4
GQA_KV_HEADS = 2
GQA_DIM = 64
HEAD_W = 64
LANES = 128
BF16_SUBLANES = 16
VT_ROWS = HEAD_W + BF16_SUBLANES
ALPHA = (2.0 * DEPTH) ** 0.25
LOG2E = math.log2(math.e)

TM = 256
KV_BLOCK = 512
ATTN_UNROLL = 4
FF_CHUNKS = ((0, 1024), (1024, 1024), (2048, 768))
IN_W = 2688
MOD_ROWS = 16
MOD_TN = 1152
VMEM_LIMIT = 56 * 1024 * 1024


def _cparams(sem):
    return pltpu.CompilerParams(dimension_semantics=sem, vmem_limit_bytes=VMEM_LIMIT)


def _const_spec(shape):
    nd = len(shape)
    return pl.BlockSpec(shape, lambda b, j: (0,) * nd, pipeline_mode=pl.Buffered(1))


def _mod_spec(has_ctx=True):
    if has_ctx:
        return pl.BlockSpec((1, N_MOD, D_MODEL), lambda b, j: (jnp.where(j == 0, 0, b + 1), 0, 0))
    return pl.BlockSpec((1, N_MOD, D_MODEL), lambda b, j: (b + 1, 0, 0))


def _layer_norm(x):
    mu = jnp.mean(x, axis=-1, keepdims=True)
    xc = x - mu
    var = jnp.mean(xc * xc, axis=-1, keepdims=True)
    return xc * lax.rsqrt(var + EPS)


def _group_mean(x, gm):
    hi = x.astype(BF16)
    lo = (x - hi.astype(F32)).astype(BF16)
    return (jnp.dot(hi, gm, preferred_element_type=F32) + jnp.dot(lo, gm, preferred_element_type=F32))


def _mod_kernel(c_ref, w_ref, b_ref, o_ref):
    c = c_ref[...]
    s = c * jax.nn.sigmoid(c)
    o_ref[0] = jnp.dot(s, w_ref[0], preferred_element_type=F32,
                       precision=lax.Precision.HIGHEST) + b_ref[0]


def _modulation(cc, w_ada, b_ada):
    depth = w_ada.shape[0]
    n = N_MOD * D_MODEL
    return pl.pallas_call(
        _mod_kernel,
        grid=(depth, n // MOD_TN),
        in_specs=[pl.BlockSpec((MOD_ROWS, D_MODEL), lambda l, j: (0, 0)),
                  pl.BlockSpec((1, D_MODEL, MOD_TN), lambda l, j: (l, 0, j)),
                  pl.BlockSpec((1, 1, MOD_TN), lambda l, j: (l, 0, j))],
        out_specs=pl.BlockSpec((1, MOD_ROWS, MOD_TN), lambda l, j: (l, 0, j)),
        out_shape=jax.ShapeDtypeStruct((depth, MOD_ROWS, n), F32),
        compiler_params=_cparams(("parallel", "parallel")),
        name="adaln_mod",
    )(cc, w_ada, b_ada.reshape(depth, 1, n))


def _ffn_kernel(*refs, s, split_input):
    if split_input:
        ctx_ref, x_ref, mod_ref, wi_ref, wo_ref, g_ref, b_ref, o_ref = refs
        x = jnp.where(pl.program_id(1) == 0, ctx_ref[0], x_ref[0])
    else:
        x_ref, mod_ref, wi_ref, wo_ref, g_ref, b_ref, o_ref = refs
        x = x_ref[0]
    mod = mod_ref[0]
    h = (_layer_norm(x) * (1.0 + mod[3 * s + 1:3 * s + 2]) + mod[3 * s:3 * s + 1]).astype(BF16)
    acc = jnp.zeros((x.shape[0], D_MODEL), F32)
    for c0, cw in FF_CHUNKS:
        gate = jnp.dot(h, wi_ref[:, c0:c0 + cw], preferred_element_type=F32)
        up = jnp.dot(h, wi_ref[:, D_FF + c0:D_FF + c0 + cw], preferred_element_type=F32)
        act = (gate * jax.nn.sigmoid(gate) * up).astype(BF16)
        acc = acc + jnp.dot(act, wo_ref[c0:c0 + cw, :], preferred_element_type=F32)
    y = ALPHA * x + (0.5 * mod[3 * s + 2:3 * s + 3]) * acc
    o_ref[0] = _layer_norm(y) * g_ref[...] + b_ref[...]


def _ffn(xs, mod, wi, wo, g, b, s, has_ctx=True):
    split_input = isinstance(xs, tuple)
    tile = pl.BlockSpec((1, TM, D_MODEL), lambda b_, j: (b_, j, 0))
    if split_input:
        ctx, x = xs
        B, S = x.shape[0], ctx.shape[1] + x.shape[1]
        data = (ctx, x)
        data_specs = [pl.BlockSpec((1, TM, D_MODEL), lambda b_, j: (b_, 0, 0)),
                      pl.BlockSpec((1, TM, D_MODEL), lambda b_, j: (b_, jnp.maximum(j - 1, 0), 0))]
    else:
        B, S, _ = xs.shape
        data = (xs,)
        data_specs = [tile]
    return pl.pallas_call(
        functools.partial(_ffn_kernel, s=s, split_input=split_input),
        grid=(B, S // TM),
        in_specs=data_specs + [_mod_spec(has_ctx), _const_spec((D_MODEL, 2 * D_FF)),
                               _const_spec((D_FF, D_MODEL)), _const_spec((1, D_MODEL)),
                               _const_spec((1, D_MODEL))],
        out_specs=tile,
        out_shape=jax.ShapeDtypeStruct((B, S, D_MODEL), F32),
        compiler_params=_cparams(("parallel", "parallel")),
        name="macaron_ffn",
    )(*data, mod, wi, wo, g.reshape(1, D_MODEL), b.reshape(1, D_MODEL))


def _rope(x, cos, sin_signed, half):
    w = x.shape[1]
    reps = w // LANES
    if reps > 1:
        cos = jnp.concatenate([cos] * reps, axis=1)
        sin_signed = jnp.concatenate([sin_signed] * reps, axis=1)
    lane = lax.broadcasted_iota(jnp.int32, x.shape, 1)
    first = (lane % (2 * half)) < half
    partner = jnp.where(first, pltpu.roll(x, w - half, 1), pltpu.roll(x, half, 1))
    return x * cos + partner * sin_signed


def _log_sigmoid(x):
    return jnp.minimum(x, 0.0) - jnp.log1p(jnp.exp(-jnp.abs(x)))


def _values_t_with_ones(v, n_heads):
    vt = v.T
    ones = jnp.ones((VT_ROWS - HEAD_W, v.shape[0]), v.dtype)
    parts = []
    for h in range(n_heads):
        parts += [vt[h * HEAD_W:(h + 1) * HEAD_W], ones]
    return jnp.concatenate(parts, axis=0)


def _mlstm_gate_slab(graw):
    t = graw.shape[0]
    lane = lax.broadcasted_iota(jnp.int32, graw.shape, 1)
    is_forget = ((lane // ML_HEADS) % 2) == 1
    gl = jnp.where(is_forget, _log_sigmoid(graw), graw)
    row = lax.broadcasted_iota(jnp.int32, (t, t), 0)
    col = lax.broadcasted_iota(jnp.int32, (t, t), 1)
    tri = jnp.where(col <= row, 1.0, 0.0).astype(BF16)
    hi = gl.astype(BF16)
    r1 = gl - hi.astype(F32)
    mid = r1.astype(BF16)
    lo = (r1 - mid.astype(F32)).astype(BF16)
    csum = (jnp.dot(tri, hi, preferred_element_type=F32) + jnp.dot(tri, mid, preferred_element_type=F32)
            + jnp.dot(tri, lo, preferred_element_type=F32))
    rsum = csum[t - 1:t] - csum + gl
    cum = jnp.where(lane >= 2 * ML_HEADS, rsum, csum)
    return jnp.where(lane < ML_GATE_COLS, gl, pltpu.roll(cum, ML_GATE_COLS, 1))


def _in_kernel(x_ref, mod_ref, w_ref, cos_a, sin_a, cos_d, sin_d, qg_ref, kg_ref, gb_ref, gm_ref,
               qat_ref, ka_ref, vat_ref, u_ref, cqt_ref, ck_ref, cvt_ref, co_ref, g_ref, gt_ref,
               qdt_ref, kd_ref, vdt_ref):
    x = x_ref[0]
    mod = mod_ref[0]
    h = (_layer_norm(x) * (1.0 + mod[4:5]) + mod[3:4]).astype(BF16)
    y = jnp.dot(h, w_ref[...], preferred_element_type=F32)
    gw = GROUP_W
    ca, sa = cos_a[...], sin_a[...]
    qa = _rope(y[:, 0:gw], ca, sa, DA_QK // 2) * (DA_QK ** -0.5 * LOG2E)
    qat_ref[0] = qa.T.astype(BF16)
    ka_ref[0] = _rope(y[:, gw:2 * gw], ca, sa, DA_QK // 2).astype(BF16)
    vat_ref[0] = _values_t_with_ones(y[:, 2 * gw:3 * gw], DA_HEADS).astype(BF16)
    u_ref[0] = y[:, 3 * gw:4 * gw]
    cqt_ref[0] = (y[:, 4 * gw:5 * gw] * ML_DIM ** -0.5).T.astype(BF16)
    ck_ref[0] = y[:, 5 * gw:6 * gw].astype(BF16)
    cvt_ref[0] = _values_t_with_ones(y[:, 6 * gw:7 * gw], ML_HEADS).astype(BF16)
    co_ref[0] = y[:, 7 * gw:8 * gw].astype(BF16)
    gates = _mlstm_gate_slab(y[:, 10 * gw:10 * gw + LANES] + gb_ref[...])
    g_ref[0] = gates
    gt_ref[0] = gates.T[0:2 * ML_GATE_COLS]
    gm = gm_ref[...]
    cd, sd = cos_d[...], sin_d[...]
    qd = y[:, 8 * gw:9 * gw]
    qd = qd * lax.rsqrt(_group_mean(qd * qd, gm) + EPS) * qg_ref[...]
    qdt_ref[0] = (_rope(qd, cd, sd, GQA_DIM // 2) * (GQA_DIM ** -0.5 * LOG2E)).T.astype(BF16)
    kvw = GQA_KV_HEADS * GQA_DIM
    kd = y[:, 9 * gw:9 * gw + kvw]
    kd = kd * lax.rsqrt(_group_mean(kd * kd, gm[:kvw, :kvw]) + EPS) * kg_ref[...]
    kd_ref[0] = _rope(kd, cd, sd, GQA_DIM // 2).astype(BF16)
    vdt_ref[0] = _values_t_with_ones(y[:, 9 * gw + kvw:9 * gw + 2 * kvw], GQA_KV_HEADS).astype(BF16)


def _in_proj(xs, mod, w_in, rope, qg, kg, gate_b, gm):
    B, S, _ = xs.shape
    gw = GROUP_W
    kvw = GQA_KV_HEADS * GQA_DIM

    def tok(width):
        return pl.BlockSpec((1, TM, width), lambda b, j: (b, j, 0))

    def tok_t(rows):
        return pl.BlockSpec((1, rows, TM), lambda b, j: (b, 0, j))

    def table():
        return pl.BlockSpec((TM, LANES), lambda b, j: (j, 0))

    out_shapes = [
        ((B, gw, S), BF16, tok_t(gw)),
        ((B, S, gw), BF16, tok(gw)),
        ((B, DA_HEADS * VT_ROWS, S), BF16, tok_t(DA_HEADS * VT_ROWS)),
        ((B, S, gw), F32, tok(gw)),
        ((B, gw, S), BF16, tok_t(gw)),
        ((B, S, gw), BF16, tok(gw)),
        ((B, ML_HEADS * VT_ROWS, S), BF16, tok_t(ML_HEADS * VT_ROWS)),
        ((B, S, gw), BF16, tok(gw)),
        ((B, S, LANES), F32, tok(LANES)),
        ((B, 2 * ML_GATE_COLS, S), F32, tok_t(2 * ML_GATE_COLS)),
        ((B, gw, S), BF16, tok_t(gw)),
        ((B, S, kvw), BF16, tok(kvw)),
        ((B, GQA_KV_HEADS * VT_ROWS, S), BF16, tok_t(GQA_KV_HEADS * VT_ROWS)),
    ]
    return pl.pallas_call(
        _in_kernel,
        grid=(B, S // TM),
        in_specs=[tok(D_MODEL), _mod_spec(), _const_spec((D_MODEL, IN_W)),
                  table(), table(), table(), table(),
                  _const_spec((1, gw)), _const_spec((1, kvw)), _const_spec((1, LANES)),
                  _const_spec((gw, gw))],
        out_specs=[o[2] for o in out_shapes],
        out_shape=[jax.ShapeDtypeStruct(o[0], o[1]) for o in out_shapes],
        compiler_params=_cparams(("parallel", "parallel")),
        name="mix_in_proj",
    )(xs, mod, w_in, rope[0], rope[1], rope[2], rope[3], qg, kg, gate_b, gm)


def _flash_absorb(st, vt, acc_ref, m_ref, c, first):
    m_blk = jnp.max(st, axis=0, keepdims=True)
    if first:
        m_new = m_blk
    else:
        m_old = m_ref[c]
        m_new = jnp.maximum(m_old, m_blk)
    p = jnp.exp2((st - m_new).astype(BF16))
    pv = jnp.dot(vt, p, preferred_element_type=F32)
    if first:
        acc_ref[c] = pv
    else:
        acc_ref[c] = jnp.exp2(m_old - m_new) * acc_ref[c] + pv
    m_ref[c] = m_new


def _flash_sweep(k_ref, vt_ref, v_row_of_chain, qw_ref, acc_ref, m_ref, st_ref):
    n_chains = len(v_row_of_chain)
    n_blk = (k_ref.shape[1] - TM) // KV_BLOCK

    def scores(k_blk, c):
        return jnp.dot(k_blk, qw_ref[c], preferred_element_type=F32)

    def vt_blk(c, start, size):
        r0 = v_row_of_chain[c]
        return vt_ref[0, r0:r0 + VT_ROWS, pl.ds(start, size)]

    def latent_start(i):
        start = TM + i * KV_BLOCK
        return start if isinstance(i, int) else pl.multiple_of(start, TM)

    def ctx_scores():
        k_ctx = k_ref[0, 0:TM, :]
        return [scores(k_ctx, c) for c in range(n_chains)]

    def absorb_ctx(sts):
        for c in range(n_chains):
            _flash_absorb(sts[c], vt_blk(c, 0, TM), acc_ref, m_ref, c, True)

    def run_half_step(i, parity):
        nxt = min(i, n_blk - 1) if isinstance(i, int) else jnp.minimum(i, n_blk - 1)
        k_blk = k_ref[0, pl.ds(latent_start(nxt), KV_BLOCK), :]
        for c in range(n_chains):
            st_ref[parity, c] = scores(k_blk, c)
            _flash_absorb(st_ref[1 - parity, c], vt_blk(c, latent_start(i - 1), KV_BLOCK), acc_ref,
                          m_ref, c, False)

    @pl.when(pl.program_id(1) == 0)
    def _():
        absorb_ctx(ctx_scores())

    @pl.when(pl.program_id(1) > 0)
    def _():
        sts = ctx_scores()
        k_blk = k_ref[0, pl.ds(latent_start(0), KV_BLOCK), :]
        for c in range(n_chains):
            st_ref[0, c] = scores(k_blk, c)
        absorb_ctx(sts)

        def body(t, _):
            for u in range(ATTN_UNROLL):
                run_half_step(t * ATTN_UNROLL + u + 1, (u + 1) % 2)
            return 0

        n_full = n_blk // ATTN_UNROLL
        lax.fori_loop(0, n_full, body, 0)
        for u in range(n_blk % ATTN_UNROLL):
            run_half_step(n_full * ATTN_UNROLL + u + 1, (u + 1) % 2)


def _attn_out_t(acc):
    return acc[:HEAD_W] * (1.0 / acc[HEAD_W:HEAD_W + 1])


def _diff_attn_kernel(qt_ref, k_ref, vt_ref, lam_ref, g_ref, o_ref, qw_ref, acc_ref, m_ref, st_ref,
                      *, lam_init):
    qt = qt_ref[0]
    feat = lax.broadcasted_iota(jnp.int32, qt.shape, 0)
    n_chains = 2 * DA_HEADS
    for c in range(n_chains):
        qw_ref[c] = jnp.where(feat // DA_QK == c, qt, jnp.zeros_like(qt))
    _flash_sweep(k_ref, vt_ref, [VT_ROWS * (c // 2) for c in range(n_chains)], qw_ref, acc_ref, m_ref,
                 st_ref)
    dl = lam_ref[...]
    lam = (jnp.exp(jnp.sum(dl[0:1] * dl[1:2], axis=1, keepdims=True))
           - jnp.exp(jnp.sum(dl[2:3] * dl[3:4], axis=1, keepdims=True)) + lam_init)
    outs = []
    for h in range(DA_HEADS):
        o = _attn_out_t(acc_ref[2 * h]) - lam * _attn_out_t(acc_ref[2 * h + 1])
        outs.append(o * lax.rsqrt(jnp.mean(o * o, axis=0, keepdims=True) + EPS))
    o_nat = jnp.concatenate(outs, axis=0).T
    o_ref[0] = (o_nat * g_ref[...] * (1.0 - lam_init)).astype(BF16)


def _gqa_kernel(qt_ref, k_ref, vt_ref, o_ref, qw_ref, acc_ref, m_ref, st_ref):
    qt = qt_ref[0]
    group = GQA_HEADS // GQA_KV_HEADS
    zeros = jnp.zeros((GQA_DIM, qt.shape[1]), qt.dtype)
    for qh in range(GQA_HEADS):
        q_rows = qt[qh * GQA_DIM:(qh + 1) * GQA_DIM]
        kvh = qh // group
        qw_ref[qh] = jnp.concatenate([q_rows if i == kvh else zeros for i in range(GQA_KV_HEADS)],
                                     axis=0)
    _flash_sweep(k_ref, vt_ref, [VT_ROWS * (qh // group) for qh in range(GQA_HEADS)],
                 qw_ref, acc_ref, m_ref, st_ref)
    o_t = jnp.concatenate([_attn_out_t(acc_ref[qh]) for qh in range(GQA_HEADS)], axis=0)
    o_ref[0] = o_t.T.astype(BF16)


def _attention(kernel_fn, qt, k, vt, extra, n_chains, name):
    B, S, kw = k.shape
    gw = qt.shape[1]
    extra_specs = [_const_spec(e.shape) for e in extra]
    return pl.pallas_call(
        kernel_fn,
        grid=(B, S // TM),
        in_specs=[pl.BlockSpec((1, gw, TM), lambda b, j: (b, 0, j)),
                  pl.BlockSpec((1, S, kw), lambda b, j: (b, 0, 0)),
                  pl.BlockSpec((1,) + vt.shape[1:], lambda b, j: (b, 0, 0))] + extra_specs,
        out_specs=pl.BlockSpec((1, TM, gw), lambda b, j: (b, j, 0)),
        out_shape=jax.ShapeDtypeStruct((B, S, gw), BF16),
        scratch_shapes=[pltpu.VMEM((n_chains, kw, TM), BF16),
                        pltpu.VMEM((n_chains, VT_ROWS, TM), F32),
                        pltpu.VMEM((n_chains, 1, TM), F32),
                        pltpu.VMEM((2, n_chains, KV_BLOCK, TM), F32)],
        compiler_params=_cparams(("parallel", "arbitrary")),
        name=name,
    )(qt, k, vt, *extra)


def _mlstm_direction(k, qt, vt, g, gt, c_ref, m_ref, d, ht_ref):
    T = k.shape[0]
    key_pos = lax.broadcasted_iota(jnp.int32, (T, T), 0)
    qry_pos = lax.broadcasted_iota(jnp.int32, (T, T), 1)
    seen = (key_pos <= qry_pos) if d == 0 else (key_pos >= qry_pos)
    feat = lax.broadcasted_iota(jnp.int32, qt.shape, 0)
    last = T - 1 if d == 0 else 0
    outs = []
    for h in range(ML_HEADS):
        li_c = 2 * ML_HEADS * d + h
        b_c = ML_GATE_COLS + li_c + ML_HEADS
        sidx = d * ML_HEADS + h
        qtm = jnp.where(feat // ML_DIM == h, qt, jnp.zeros_like(qt))
        r_col = g[:, li_c:li_c + 1] - g[:, b_c:b_c + 1]
        b_row = gt[b_c:b_c + 1, :]
        li_row = gt[li_c:li_c + 1, :]
        m_prev = m_ref[sidx:sidx + 1, 0:1]
        c_prev = c_ref[d, h]
        vth = vt[VT_ROWS * h:VT_ROWS * (h + 1)]
        st = jnp.dot(k, qtm, preferred_element_type=F32)
        dmat = jnp.where(seen, b_row + r_col, -jnp.inf)
        inter = b_row + m_prev
        m_t = jnp.maximum(inter, jnp.max(dmat, axis=0, keepdims=True))
        a = (st * jnp.exp(dmat - m_t)).astype(BF16)
        w_inter = jnp.exp(inter - m_t)
        num = (jnp.dot(vth, a, preferred_element_type=F32)
               + w_inter * jnp.dot(c_prev.astype(BF16), qtm, preferred_element_type=F32))
        den = jnp.maximum(jnp.abs(num[ML_DIM:ML_DIM + 1]), jnp.exp(-m_t))
        outs.append(num[:ML_DIM] * (1.0 / den))
        b_end = b_row[:, last:last + 1]
        g_row = b_end - b_row + li_row
        m_new = jnp.maximum(b_end + m_prev, jnp.max(g_row, axis=1, keepdims=True))
        wk = jnp.exp(g_row - m_new)
        decay = jnp.exp(b_end + m_prev - m_new)
        vw = (vth.astype(F32) * wk).astype(BF16)
        c_ref[d, h] = decay * c_prev + jnp.dot(vw, k, preferred_element_type=F32)
        m_ref[sidx:sidx + 1, :] = jnp.broadcast_to(m_new, (1, LANES))
    ht_ref[0] = jnp.concatenate(outs, axis=0)


def _mlstm_kernel(kf_ref, qf_ref, vf_ref, gf_ref, gtf_ref, kb_ref, qb_ref, vb_ref, gb_ref, gtb_ref,
                  hf_ref, hb_ref, c_ref, m_ref):
    @pl.when(pl.program_id(1) == 0)
    def _():
        c_ref[...] = jnp.zeros_like(c_ref)
        m_ref[...] = jnp.zeros_like(m_ref)

    _mlstm_direction(kf_ref[0], qf_ref[0], vf_ref[0], gf_ref[0], gtf_ref[0], c_ref, m_ref, 0, hf_ref)
    _mlstm_direction(kb_ref[0], qb_ref[0], vb_ref[0], gb_ref[0], gtb_ref[0], c_ref, m_ref, 1, hb_ref)


def _mlstm(ck, cqt, cvt, gates, gates_t):
    B, S, gw = ck.shape
    nc = S // TM

    def fwd_chunk(i):
        return i

    def bwd_chunk(i):
        return jnp.where(i == 0, 0, nc - i)

    def specs(chunk):
        def tok(width):
            return pl.BlockSpec((1, TM, width), lambda b, i: (b, chunk(i), 0))

        def tok_t(rows):
            return pl.BlockSpec((1, rows, TM), lambda b, i: (b, 0, chunk(i)))

        return [tok(gw), tok_t(gw), tok_t(cvt.shape[1]), tok(LANES), tok_t(gates_t.shape[1])], tok_t(gw)

    in_f, out_f = specs(fwd_chunk)
    in_b, out_b = specs(bwd_chunk)
    return pl.pallas_call(
        _mlstm_kernel,
        grid=(B, nc),
        in_specs=in_f + in_b,
        out_specs=[out_f, out_b],
        out_shape=[jax.ShapeDtypeStruct((B, gw, S), F32)] * 2,
        scratch_shapes=[pltpu.VMEM((2, ML_HEADS, VT_ROWS, gw), F32),
                        pltpu.VMEM((2 * ML_HEADS, LANES), F32)],
        compiler_params=_cparams(("parallel", "arbitrary")),
        name="mlstm_scan",
    )(ck, cqt, cvt, gates, gates_t, ck, cqt, cvt, gates, gates_t)


def _pool_mixer(u_ref, up_ref, un_ref, ext_ref, pw_ref, ps_ref, first_tile):
    j = pl.program_id(1) + first_tile
    nt = pl.num_programs(1) + first_tile
    has_prev = j >= 2
    has_next = jnp.logical_and(j >= 1, j < nt - 1)
    u = u_ref[0]
    ext_ref[0:POOL_HALO, :] = jnp.where(has_prev, up_ref[0], 0.0)
    ext_ref[POOL_HALO:POOL_HALO + TM, :] = u
    ext_ref[POOL_HALO + TM:, :] = jnp.where(has_next, un_ref[0], 0.0)

    def shifted(s):
        return ext_ref[POOL_HALO + s:POOL_HALO + s + TM, :]

    lane = lax.broadcasted_iota(jnp.int32, (TM, GROUP_W), 1)
    r = lax.broadcasted_iota(jnp.int32, (TM, GROUP_W), 0)
    far = 2 * POOL_HALO
    left_room = r + jnp.where(has_prev, far, 0)
    right_room = (TM - 1 - r) + jnp.where(has_next, far, 0)
    total = u
    mean = jnp.zeros_like(u)
    prev_half = 0
    for gi, w in enumerate(POOL_WINDOWS):
        half = w // 2
        for s in list(range(-half, -prev_half)) + list(range(max(prev_half, 1), half)):
            total = total + shifted(s)
        prev_half = half
        cnt = jnp.minimum(left_room, half) + jnp.minimum(right_room, half - 1) + 1
        mean = jnp.where(lane // POOL_GROUP == gi, total / cnt.astype(F32), mean)
    dlt = (mean - u).astype(BF16)
    return jnp.dot(dlt, pw_ref[...], preferred_element_type=F32) * ps_ref[...]


def _out_kernel(x_ref, mod_ref, a_ref, u_ref, up_ref, un_ref, hf_ref, hb_ref, co_ref, d_ref,
                w_ref, pw_ref, ps_ref, mg_ref, gm_ref, g_ref, b_ref, o_ref, ext_ref, *, first_tile):
    gw = GROUP_W
    x = x_ref[0]
    mod = mod_ref[0]
    b_mix = _pool_mixer(u_ref, up_ref, un_ref, ext_ref, pw_ref, ps_ref, first_tile)
    hm = (hf_ref[0] + hb_ref[0]).T
    c_mix = (hm * lax.rsqrt(_group_mean(hm * hm, gm_ref[...]) + EPS) * mg_ref[...]
             * jax.nn.sigmoid(co_ref[0].astype(F32)))
    o = (jnp.dot(a_ref[0], w_ref[0:gw, :], preferred_element_type=F32)
         + jnp.dot(b_mix.astype(BF16), w_ref[gw:2 * gw, :], preferred_element_type=F32)
         + jnp.dot(c_mix.astype(BF16), w_ref[2 * gw:3 * gw, :], preferred_element_type=F32)
         + jnp.dot(d_ref[0], w_ref[3 * gw:4 * gw, :], preferred_element_type=F32))
    y = ALPHA * x + mod[5:6] * o
    o_ref[0] = _layer_norm(y) * g_ref[...] + b_ref[...]


def _out_proj(xs, mod, a, u, hf, hb, co, dd, w_out, pool_bd, pool_scale, ml_g, gm, g, b, keep_ctx):
    B, S, _ = xs.shape
    gw = GROUP_W
    hb_per_tile = TM // POOL_HALO
    n_halo = S // POOL_HALO
    t0 = 0 if keep_ctx else 1

    def tok(width):
        return pl.BlockSpec((1, TM, width), lambda b_, j: (b_, j + t0, 0))

    def tok_t(rows):
        return pl.BlockSpec((1, rows, TM), lambda b_, j: (b_, 0, j + t0))

    prev_spec = pl.BlockSpec((1, POOL_HALO, gw),
                             lambda b_, j: (b_, jnp.maximum((j + t0) * hb_per_tile - 1, 0), 0))
    next_spec = pl.BlockSpec((1, POOL_HALO, gw),
                             lambda b_, j: (b_, jnp.minimum((j + t0 + 1) * hb_per_tile, n_halo - 1), 0))
    return pl.pallas_call(
        functools.partial(_out_kernel, first_tile=t0),
        grid=(B, S // TM - t0),
        in_specs=[tok(D_MODEL), _mod_spec(keep_ctx), tok(gw), tok(gw), prev_spec, next_spec,
                  tok_t(gw), tok_t(gw), tok(gw), tok(gw),
                  _const_spec((D_MODEL, D_MODEL)), _const_spec((gw, gw)), _const_spec((1, gw)),
                  _const_spec((1, gw)), _const_spec((gw, gw)),
                  _const_spec((1, D_MODEL)), _const_spec((1, D_MODEL))],
        out_specs=pl.BlockSpec((1, TM, D_MODEL), lambda b_, j: (b_, j, 0)),
        out_shape=jax.ShapeDtypeStruct((B, S - t0 * TM, D_MODEL), F32),
        scratch_shapes=[pltpu.VMEM((TM + 2 * POOL_HALO, gw), F32)],
        compiler_params=_cparams(("parallel", "parallel")),
        name="mix_out_proj",
    )(xs, mod, a, u, u, u, hf, hb, co, dd, w_out, pool_bd, pool_scale, ml_g, gm, g, b)


def _rope_tables(L, Lc, dim):
    rows = L // GRID_W
    row = jnp.repeat(jnp.arange(rows), GRID_W).astype(F32)
    col = jnp.tile(jnp.arange(GRID_W), rows).astype(F32)
    axis_dim = dim // 2
    inv = ROPE_THETA ** (-jnp.arange(0, axis_dim, 2, dtype=F32) / axis_dim)
    ang = jnp.concatenate([row[:, None] * inv, col[:, None] * inv], axis=-1)
    cos, sin = jnp.cos(ang), jnp.sin(ang)
    reps = LANES // dim
    cos_p = jnp.tile(jnp.concatenate([cos, cos], axis=-1), (1, reps))
    sin_p = jnp.tile(jnp.concatenate([-sin, sin], axis=-1), (1, reps))
    cos_p = jnp.concatenate([jnp.ones((Lc, LANES), F32), cos_p], axis=0)
    sin_p = jnp.concatenate([jnp.zeros((Lc, LANES), F32), sin_p], axis=0)
    return cos_p, sin_p


def _reorder_w_in(w):
    g0 = 8 * GROUP_W
    g1 = g0 + 4 * ML_HEADS
    pad = jnp.zeros((w.shape[0], LANES - 4 * ML_HEADS), w.dtype)
    return jnp.concatenate([w[:, :g0], w[:, g1:], w[:, g0:g1], pad], axis=1).astype(BF16)


def _block_diag(blocks):
    n = blocks.shape[0]
    rows = []
    for i in range(n):
        rows.append(jnp.concatenate([blocks[i] if k == i else jnp.zeros_like(blocks[i])
                                     for k in range(n)], axis=1))
    return jnp.concatenate(rows, axis=0)


def kernel(x, c, ctx, c_ctx, w_ada, b_ada, ln_g, ln_b, ffn1_wi, ffn1_wo, ffn2_wi, ffn2_wo, w_in, w_out,
           diff_lambda, diff_norm_g, pool_w, pool_scale, ml_gate_b, ml_norm_g, gqa_qnorm_g, gqa_knorm_g):
    B, L, _ = x.shape
    Lc = ctx.shape[1]
    assert Lc == TM and L % KV_BLOCK == 0 and L % GRID_W == 0 and B + 1 <= MOD_ROWS
    depth = w_ada.shape[0]

    cc = jnp.concatenate([c_ctx[None], c, jnp.zeros((MOD_ROWS - 1 - B, D_MODEL), F32)], axis=0)
    mod_all = _modulation(cc, w_ada, b_ada).reshape(depth, MOD_ROWS, N_MOD, D_MODEL)

    rope = _rope_tables(L, Lc, DA_QK) + _rope_tables(L, Lc, GQA_DIM)
    gm = _block_diag(jnp.full((GROUP_W // HEAD_W, HEAD_W, HEAD_W), 1.0 / HEAD_W, BF16))

    xs = (ctx, x)
    for l in range(depth):
        last = l == depth - 1
        mod = mod_all[l]
        lam_init = 0.8 - 0.6 * math.exp(-0.3 * l)
        xs = _ffn(xs, mod, ffn1_wi[l].astype(BF16), ffn1_wo[l].astype(BF16), ln_g[l, 0], ln_b[l, 0], 0)
        gate_b = jnp.concatenate([ml_gate_b[l].reshape(1, -1),
                                  jnp.zeros((1, LANES - 4 * ML_HEADS), F32)], axis=1)
        qg = jnp.tile(gqa_qnorm_g[l], GQA_HEADS).reshape(1, -1)
        kg = jnp.tile(gqa_knorm_g[l], GQA_KV_HEADS).reshape(1, -1)
        (qat, ka, vat, u, cqt, ck, cvt, co, gates, gates_t, qdt, kd, vdt) = _in_proj(
            xs, mod, _reorder_w_in(w_in[l]), rope, qg, kg, gate_b, gm)
        a = _attention(functools.partial(_diff_attn_kernel, lam_init=lam_init), qat, ka, vat,
                       [diff_lambda[l], diff_norm_g[l].reshape(1, -1)], 2 * DA_HEADS, "diff_attention")
        dd = _attention(_gqa_kernel, qdt, kd, vdt, [], GQA_HEADS, "gqa_attention")
        hf, hb = _mlstm(ck, cqt, cvt, gates, gates_t)
        xs = _out_proj(xs, mod, a, u, hf, hb, co, dd, w_out[l].astype(BF16),
                       _block_diag(pool_w[l]).astype(BF16), pool_scale[l].reshape(1, -1),
                       ml_norm_g[l].reshape(1, -1), gm, ln_g[l, 1].reshape(1, -1),
                       ln_b[l, 1].reshape(1, -1), keep_ctx=not last)
        xs = _ffn(xs, mod, ffn2_wi[l].astype(BF16), ffn2_wo[l].astype(BF16), ln_g[l, 2], ln_b[l, 2], 2,
                  has_ctx=not last)
    return xs
```

```python
import functools
import math

import jax
import jax.numpy as jnp
from jax import lax
from jax.experimental import pallas as pl
from jax.experimental.pallas import tpu as pltpu

F32 = jnp.float32
BF16 = jnp.bfloat16

D_MODEL = 1024
DEPTH = 2
GRID_W = 64
GROUP_W = 256
D_FF = 2816
N_MOD = 9
EPS = 1e-6
ROPE_THETA = 10000.0
DA_HEADS = 4
DA_QK = 32
DA_V = 64
POOL_WINDOWS = (2, 4, 8, 16)
POOL_GROUP = 64
POOL_HALO = 8
ML_HEADS = 4
ML_DIM = 64
ML_GATE_COLS = 4 * ML_HEADS
GQA_HEADS = 4
GQA_KV_HEADS = 2
GQA_DIM = 64
HEAD_W = 64
LANES = 128
BF16_SUBLANES = 16
VT_ROWS = HEAD_W + BF16_SUBLANES
ALPHA = (2.0 * DEPTH) ** 0.25
LOG2E = math.log2(math.e)

TM = 256
KV_BLOCK = 512
ATTN_UNROLL = 4
FF_CHUNKS = ((0, 1024), (1024, 1024), (2048, 768))
IN_W = 2688
MOD_ROWS = 16
MOD_TN = 1152
VMEM_LIMIT = 56 * 1024 * 1024


def _cparams(sem):
    return pltpu.CompilerParams(dimension_semantics=sem, vmem_limit_bytes=VMEM_LIMIT)


def _const_spec(shape):
    nd = len(shape)
    return pl.BlockSpec(shape, lambda b, j: (0,) * nd, pipeline_mode=pl.Buffered(1))


def _mod_spec(has_ctx=True):
    if has_ctx:
        return pl.BlockSpec((1, N_MOD, D_MODEL), lambda b, j: (jnp.where(j == 0, 0, b + 1), 0, 0))
    return pl.BlockSpec((1, N_MOD, D_MODEL), lambda b, j: (b + 1, 0, 0))


def _layer_norm(x):
    mu = jnp.mean(x, axis=-1, keepdims=True)
    xc = x - mu
    var = jnp.mean(xc * xc, axis=-1, keepdims=True)
    return xc * lax.rsqrt(var + EPS)


def _group_mean(x, gm):
    hi = x.astype(BF16)
    lo = (x - hi.astype(F32)).astype(BF16)
    return (jnp.dot(hi, gm, preferred_element_type=F32) + jnp.dot(lo, gm, preferred_element_type=F32))


def _mod_kernel(c_ref, w_ref, b_ref, o_ref):
    c = c_ref[...]
    s = c * jax.nn.sigmoid(c)
    o_ref[0] = jnp.dot(s, w_ref[0], preferred_element_type=F32,
                       precision=lax.Precision.HIGHEST) + b_ref[0]


def _modulation(cc, w_ada, b_ada):
    depth = w_ada.shape[0]
    n = N_MOD * D_MODEL
    return pl.pallas_call(
        _mod_kernel,
        grid=(depth, n // MOD_TN),
        in_specs=[pl.BlockSpec((MOD_ROWS, D_MODEL), lambda l, j: (0, 0)),
                  pl.BlockSpec((1, D_MODEL, MOD_TN), lambda l, j: (l, 0, j)),
                  pl.BlockSpec((1, 1, MOD_TN), lambda l, j: (l, 0, j))],
        out_specs=pl.BlockSpec((1, MOD_ROWS, MOD_TN), lambda l, j: (l, 0, j)),
        out_shape=jax.ShapeDtypeStruct((depth, MOD_ROWS, n), F32),
        compiler_params=_cparams(("parallel", "parallel")),
        name="adaln_mod",
    )(cc, w_ada, b_ada.reshape(depth, 1, n))


def _ffn_kernel(*refs, s, split_input, tiles_per_sample):
    n_data = 4 if split_input else 1
    data = refs[:n_data]
    mod0_ref, mod1_ref, wi_ref, wo_ref, g_ref, b_ref, o_ref = refs[n_data:]
    xs = []
    for half in range(2):
        if split_input:
            j = (2 * pl.program_id(0) + half) % tiles_per_sample
            xs.append(jnp.where(j == 0, data[2 * half][0], data[2 * half + 1][0]))
        else:
            xs.append(data[0][half])
    mods = [mod0_ref[0], mod1_ref[0]]
    hs = [(_layer_norm(x) * (1.0 + mod[3 * s + 1:3 * s + 2]) + mod[3 * s:3 * s + 1]).astype(BF16)
          for x, mod in zip(xs, mods)]
    accs = []
    for h in hs:
        acc = jnp.zeros((TM, D_MODEL), F32)
        for c0, cw in FF_CHUNKS:
            gate = jnp.dot(h, wi_ref[:, c0:c0 + cw], preferred_element_type=F32)
            up = jnp.dot(h, wi_ref[:, D_FF + c0:D_FF + c0 + cw], preferred_element_type=F32)
            act = (gate * jax.nn.sigmoid(gate) * up).astype(BF16)
            acc = acc + jnp.dot(act, wo_ref[c0:c0 + cw, :], preferred_element_type=F32)
        accs.append(acc)
    for half, (x, mod, acc) in enumerate(zip(xs, mods, accs)):
        y = ALPHA * x + (0.5 * mod[3 * s + 2:3 * s + 3]) * acc
        o_ref[half] = _layer_norm(y) * g_ref[...] + b_ref[...]


def _ffn(xs, mod, wi, wo, g, b, s, has_ctx=True):
    split_input = isinstance(xs, tuple)
    if split_input:
        ctx, x = xs
        B, S = x.shape[0], ctx.shape[1] + x.shape[1]
    else:
        B, S, _ = xs.shape
    n = S // TM
    assert (B * n) % 2 == 0

    def sample_tile(p, half):
        sub = 2 * p + half
        return sub // n, sub % n

    def mod_spec(half):
        def index(p):
            b_, j = sample_tile(p, half)
            return (jnp.where(jnp.logical_and(has_ctx, j == 0), 0, b_ + 1), 0, 0)
        return pl.BlockSpec((1, N_MOD, D_MODEL), index)

    def const(shape):
        return pl.BlockSpec(shape, lambda p: (0,) * len(shape), pipeline_mode=pl.Buffered(1))

    if split_input:
        data, data_specs = [], []
        for half in range(2):
            data += [ctx, x]
            data_specs += [
                pl.BlockSpec((1, TM, D_MODEL), lambda p, h=half: (sample_tile(p, h)[0], 0, 0)),
                pl.BlockSpec((1, TM, D_MODEL),
                             lambda p, h=half: (sample_tile(p, h)[0],
                                                jnp.maximum(sample_tile(p, h)[1] - 1, 0), 0))]
    else:
        data = [xs.reshape(B * n, TM, D_MODEL)]
        data_specs = [pl.BlockSpec((2, TM, D_MODEL), lambda p: (p, 0, 0))]
    out = pl.pallas_call(
        functools.partial(_ffn_kernel, s=s, split_input=split_input, tiles_per_sample=n),
        grid=(B * n // 2,),
        in_specs=data_specs + [mod_spec(0), mod_spec(1), const((D_MODEL, 2 * D_FF)),
                               const((D_FF, D_MODEL)), const((1, D_MODEL)), const((1, D_MODEL))],
        out_specs=pl.BlockSpec((2, TM, D_MODEL), lambda p: (p, 0, 0)),
        out_shape=jax.ShapeDtypeStruct((B * n, TM, D_MODEL), F32),
        compiler_params=_cparams(("parallel",)),
        name="macaron_ffn",
    )(*data, mod, mod, wi, wo, g.reshape(1, D_MODEL), b.reshape(1, D_MODEL))
    return out.reshape(B, S, D_MODEL)


def _rope(x, cos, sin_signed, half):
    w = x.shape[1]
    reps = w // LANES
    if reps > 1:
        cos = jnp.concatenate([cos] * reps, axis=1)
        sin_signed = jnp.concatenate([sin_signed] * reps, axis=1)
    lane = lax.broadcasted_iota(jnp.int32, x.shape, 1)
    first = (lane % (2 * half)) < half
    partner = jnp.where(first, pltpu.roll(x, w - half, 1), pltpu.roll(x, half, 1))
    return x * cos + partner * sin_signed


def _log_sigmoid(x):
    return jnp.minimum(x, 0.0) - jnp.log1p(jnp.exp(-jnp.abs(x)))


def _values_t_with_ones(v, n_heads):
    vt = v.T
    ones = jnp.ones((VT_ROWS - HEAD_W, v.shape[0]), v.dtype)
    parts = []
    for h in range(n_heads):
        parts += [vt[h * HEAD_W:(h + 1) * HEAD_W], ones]
    return jnp.concatenate(parts, axis=0)


def _mlstm_gate_slab(graw):
    t = graw.shape[0]
    lane = lax.broadcasted_iota(jnp.int32, graw.shape, 1)
    is_forget = ((lane // ML_HEADS) % 2) == 1
    gl = jnp.where(is_forget, _log_sigmoid(graw), graw)
    row = lax.broadcasted_iota(jnp.int32, (t, t), 0)
    col = lax.broadcasted_iota(jnp.int32, (t, t), 1)
    tri = jnp.where(col <= row, 1.0, 0.0).astype(BF16)
    hi = gl.astype(BF16)
    r1 = gl - hi.astype(F32)
    mid = r1.astype(BF16)
    lo = (r1 - mid.astype(F32)).astype(BF16)
    csum = (jnp.dot(tri, hi, preferred_element_type=F32) + jnp.dot(tri, mid, preferred_element_type=F32)
            + jnp.dot(tri, lo, preferred_element_type=F32))
    rsum = csum[t - 1:t] - csum + gl
    cum = jnp.where(lane >= 2 * ML_HEADS, rsum, csum)
    return jnp.where(lane < ML_GATE_COLS, gl, pltpu.roll(cum, ML_GATE_COLS, 1))


def _in_kernel(x_ref, mod_ref, w_ref, cos_a, sin_a, cos_d, sin_d, qg_ref, kg_ref, gb_ref, gm_ref,
               qat_ref, ka_ref, vat_ref, u_ref, cqt_ref, ck_ref, cvt_ref, co_ref, g_ref, gt_ref,
               qdt_ref, kd_ref, vdt_ref):
    x = x_ref[0]
    mod = mod_ref[0]
    h = (_layer_norm(x) * (1.0 + mod[4:5]) + mod[3:4]).astype(BF16)
    y = jnp.dot(h, w_ref[...], preferred_element_type=F32)
    gw = GROUP_W
    ca, sa = cos_a[...], sin_a[...]
    qa = _rope(y[:, 0:gw], ca, sa, DA_QK // 2) * (DA_QK ** -0.5 * LOG2E)
    qat_ref[0] = qa.T.astype(BF16)
    ka_ref[0] = _rope(y[:, gw:2 * gw], ca, sa, DA_QK // 2).astype(BF16)
    vat_ref[0] = _values_t_with_ones(y[:, 2 * gw:3 * gw], DA_HEADS).astype(BF16)
    u_ref[0] = y[:, 3 * gw:4 * gw]
    cqt_ref[0] = (y[:, 4 * gw:5 * gw] * ML_DIM ** -0.5).T.astype(BF16)
    ck_ref[0] = y[:, 5 * gw:6 * gw].astype(BF16)
    cvt_ref[0] = _values_t_with_ones(y[:, 6 * gw:7 * gw], ML_HEADS).astype(BF16)
    co_ref[0] = y[:, 7 * gw:8 * gw].astype(BF16)
    gates = _mlstm_gate_slab(y[:, 10 * gw:10 * gw + LANES] + gb_ref[...])
    g_ref[0] = gates
    gt_ref[0] = gates.T[0:2 * ML_GATE_COLS]
    gm = gm_ref[...]
    cd, sd = cos_d[...], sin_d[...]
    qd = y[:, 8 * gw:9 * gw]
    qd = qd * lax.rsqrt(_group_mean(qd * qd, gm) + EPS) * qg_ref[...]
    qdt_ref[0] = (_rope(qd, cd, sd, GQA_DIM // 2) * (GQA_DIM ** -0.5 * LOG2E)).T.astype(BF16)
    kvw = GQA_KV_HEADS * GQA_DIM
    kd = y[:, 9 * gw:9 * gw + kvw]
    kd = kd * lax.rsqrt(_group_mean(kd * kd, gm[:kvw, :kvw]) + EPS) * kg_ref[...]
    kd_ref[0] = _rope(kd, cd, sd, GQA_DIM // 2).astype(BF16)
    vdt_ref[0] = _values_t_with_ones(y[:, 9 * gw + kvw:9 * gw + 2 * kvw], GQA_KV_HEADS).astype(BF16)


def _in_proj(xs, mod, w_in, rope, qg, kg, gate_b, gm):
    B, S, _ = xs.shape
    gw = GROUP_W
    kvw = GQA_KV_HEADS * GQA_DIM

    def tok(width):
        return pl.BlockSpec((1, TM, width), lambda b, j: (b, j, 0))

    def tok_t(rows):
        return pl.BlockSpec((1, rows, TM), lambda b, j: (b, 0, j))

    def table():
        return pl.BlockSpec((TM, LANES), lambda b, j: (j, 0))

    out_shapes = [
        ((B, gw, S), BF16, tok_t(gw)),
        ((B, S, gw), BF16, tok(gw)),
        ((B, DA_HEADS * VT_ROWS, S), BF16, tok_t(DA_HEADS * VT_ROWS)),
        ((B, S, gw), F32, tok(gw)),
        ((B, gw, S), BF16, tok_t(gw)),
        ((B, S, gw), BF16, tok(gw)),
        ((B, ML_HEADS * VT_ROWS, S), BF16, tok_t(ML_HEADS * VT_ROWS)),
        ((B, S, gw), BF16, tok(gw)),
        ((B, S, LANES), F32, tok(LANES)),
        ((B, 2 * ML_GATE_COLS, S), F32, tok_t(2 * ML_GATE_COLS)),
        ((B, gw, S), BF16, tok_t(gw)),
        ((B, S, kvw), BF16, tok(kvw)),
        ((B, GQA_KV_HEADS * VT_ROWS, S), BF16, tok_t(GQA_KV_HEADS * VT_ROWS)),
    ]
    return pl.pallas_call(
        _in_kernel,
        grid=(B, S // TM),
        in_specs=[tok(D_MODEL), _mod_spec(), _const_spec((D_MODEL, IN_W)),
                  table(), table(), table(), table(),
                  _const_spec((1, gw)), _const_spec((1, kvw)), _const_spec((1, LANES)),
                  _const_spec((gw, gw))],
        out_specs=[o[2] for o in out_shapes],
        out_shape=[jax.ShapeDtypeStruct(o[0], o[1]) for o in out_shapes],
        compiler_params=_cparams(("parallel", "parallel")),
        name="mix_in_proj",
    )(xs, mod, w_in, rope[0], rope[1], rope[2], rope[3], qg, kg, gate_b, gm)


def _flash_absorb(st, m_blk, vt, acc_ref, m_ref, c, first):
    if first:
        m_new = m_blk
    else:
        m_old = m_ref[c]
        m_new = jnp.maximum(m_old, m_blk)
    p = jnp.exp2((st - m_new).astype(BF16))
    pv = jnp.dot(vt, p, preferred_element_type=F32)
    if first:
        acc_ref[c] = pv
    else:
        acc_ref[c] = jnp.exp2(m_old - m_new) * acc_ref[c] + pv
    m_ref[c] = m_new


def _flash_sweep(k_ref, vt_ref, v_row_of_chain, qw_ref, acc_ref, m_ref, st_ref, mb_ref):
    n_chains = len(v_row_of_chain)
    n_blk = (k_ref.shape[1] - TM) // KV_BLOCK

    def scores(k_blk, c):
        return jnp.dot(k_blk, qw_ref[c], preferred_element_type=F32)

    def vt_blk(c, start, size):
        r0 = v_row_of_chain[c]
        return vt_ref[0, r0:r0 + VT_ROWS, pl.ds(start, size)]

    def latent_start(i):
        start = TM + i * KV_BLOCK
        return start if isinstance(i, int) else pl.multiple_of(start, TM)

    def ctx_scores():
        k_ctx = k_ref[0, 0:TM, :]
        return [scores(k_ctx, c) for c in range(n_chains)]

    def absorb_ctx(sts):
        for c in range(n_chains):
            _flash_absorb(sts[c], jnp.max(sts[c], axis=0, keepdims=True), vt_blk(c, 0, TM), acc_ref,
                          m_ref, c, True)

    def produce_scores(k_blk, slot, c):
        st = scores(k_blk, c)
        st_ref[slot, c] = st
        mb_ref[slot, c] = jnp.max(st, axis=0, keepdims=True)

    def run_half_step(i, parity):
        nxt = min(i, n_blk - 1) if isinstance(i, int) else jnp.minimum(i, n_blk - 1)
        k_blk = k_ref[0, pl.ds(latent_start(nxt), KV_BLOCK), :]
        for c in range(n_chains):
            produce_scores(k_blk, parity, c)
            _flash_absorb(st_ref[1 - parity, c], mb_ref[1 - parity, c],
                          vt_blk(c, latent_start(i - 1), KV_BLOCK), acc_ref, m_ref, c, False)

    @pl.when(pl.program_id(1) == 0)
    def _():
        absorb_ctx(ctx_scores())

    @pl.when(pl.program_id(1) > 0)
    def _():
        sts = ctx_scores()
        k_blk = k_ref[0, pl.ds(latent_start(0), KV_BLOCK), :]
        for c in range(n_chains):
            produce_scores(k_blk, 0, c)
        absorb_ctx(sts)

        def body(t, _):
            for u in range(ATTN_UNROLL):
                run_half_step(t * ATTN_UNROLL + u + 1, (u + 1) % 2)
            return 0

        n_full = n_blk // ATTN_UNROLL
        lax.fori_loop(0, n_full, body, 0)
        for u in range(n_blk % ATTN_UNROLL):
            run_half_step(n_full * ATTN_UNROLL + u + 1, (u + 1) % 2)


def _attn_out_t(acc):
    return acc[:HEAD_W] * (1.0 / acc[HEAD_W:HEAD_W + 1])


def _diff_attn_kernel(qt_ref, k_ref, vt_ref, lam_ref, g_ref, o_ref, qw_ref, acc_ref, m_ref, st_ref,
                      mb_ref, *, lam_init):
    qt = qt_ref[0]
    feat = lax.broadcasted_iota(jnp.int32, qt.shape, 0)
    n_chains = 2 * DA_HEADS
    for c in range(n_chains):
        qw_ref[c] = jnp.where(feat // DA_QK == c, qt, jnp.zeros_like(qt))
    _flash_sweep(k_ref, vt_ref, [VT_ROWS * (c // 2) for c in range(n_chains)], qw_ref, acc_ref, m_ref,
                 st_ref, mb_ref)
    dl = lam_ref[...]
    lam = (jnp.exp(jnp.sum(dl[0:1] * dl[1:2], axis=1, keepdims=True))
           - jnp.exp(jnp.sum(dl[2:3] * dl[3:4], axis=1, keepdims=True)) + lam_init)
    outs = []
    for h in range(DA_HEADS):
        o = _attn_out_t(acc_ref[2 * h]) - lam * _attn_out_t(acc_ref[2 * h + 1])
        outs.append(o * lax.rsqrt(jnp.mean(o * o, axis=0, keepdims=True) + EPS))
    o_nat = jnp.concatenate(outs, axis=0).T
    o_ref[0] = (o_nat * g_ref[...] * (1.0 - lam_init)).astype(BF16)


def _gqa_kernel(qt_ref, k_ref, vt_ref, o_ref, qw_ref, acc_ref, m_ref, st_ref, mb_ref):
    qt = qt_ref[0]
    group = GQA_HEADS // GQA_KV_HEADS
    zeros = jnp.zeros((GQA_DIM, qt.shape[1]), qt.dtype)
    for qh in range(GQA_HEADS):
        q_rows = qt[qh * GQA_DIM:(qh + 1) * GQA_DIM]
        kvh = qh // group
        qw_ref[qh] = jnp.concatenate([q_rows if i == kvh else zeros for i in range(GQA_KV_HEADS)],
                                     axis=0)
    _flash_sweep(k_ref, vt_ref, [VT_ROWS * (qh // group) for qh in range(GQA_HEADS)],
                 qw_ref, acc_ref, m_ref, st_ref, mb_ref)
    o_t = jnp.concatenate([_attn_out_t(acc_ref[qh]) for qh in range(GQA_HEADS)], axis=0)
    o_ref[0] = o_t.T.astype(BF16)


def _attention(kernel_fn, qt, k, vt, extra, n_chains, name):
    B, S, kw = k.shape
    gw = qt.shape[1]
    extra_specs = [_const_spec(e.shape) for e in extra]
    return pl.pallas_call(
        kernel_fn,
        grid=(B, S // TM),
        in_specs=[pl.BlockSpec((1, gw, TM), lambda b, j: (b, 0, j)),
                  pl.BlockSpec((1, S, kw), lambda b, j: (b, 0, 0)),
                  pl.BlockSpec((1,) + vt.shape[1:], lambda b, j: (b, 0, 0))] + extra_specs,
        out_specs=pl.BlockSpec((1, TM, gw), lambda b, j: (b, j, 0)),
        out_shape=jax.ShapeDtypeStruct((B, S, gw), BF16),
        scratch_shapes=[pltpu.VMEM((n_chains, kw, TM), BF16),
                        pltpu.VMEM((n_chains, VT_ROWS, TM), F32),
                        pltpu.VMEM((n_chains, 1, TM), F32),
                        pltpu.VMEM((2, n_chains, KV_BLOCK, TM), F32),
                        pltpu.VMEM((2, n_chains, 1, TM), F32)],
        compiler_params=_cparams(("parallel", "arbitrary")),
        name=name,
    )(qt, k, vt, *extra)


def _mlstm_direction(k, qt, vt, g, gt, c_ref, m_ref, d, ht_ref):
    T = k.shape[0]
    key_pos = lax.broadcasted_iota(jnp.int32, (T, T), 0)
    qry_pos = lax.broadcasted_iota(jnp.int32, (T, T), 1)
    seen = (key_pos <= qry_pos) if d == 0 else (key_pos >= qry_pos)
    feat = lax.broadcasted_iota(jnp.int32, qt.shape, 0)
    last = T - 1 if d == 0 else 0
    outs = []
    for h in range(ML_HEADS):
        li_c = 2 * ML_HEADS * d + h
        b_c = ML_GATE_COLS + li_c + ML_HEADS
        sidx = d * ML_HEADS + h
        qtm = jnp.where(feat // ML_DIM == h, qt, jnp.zeros_like(qt))
        r_col = g[:, li_c:li_c + 1] - g[:, b_c:b_c + 1]
        b_row = gt[b_c:b_c + 1, :]
        li_row = gt[li_c:li_c + 1, :]
        m_prev = m_ref[sidx:sidx + 1, 0:1]
        c_prev = c_ref[d, h]
        vth = vt[VT_ROWS * h:VT_ROWS * (h + 1)]
        st = jnp.dot(k, qtm, preferred_element_type=F32)
        dmat = jnp.where(seen, b_row + r_col, -jnp.inf)
        inter = b_row + m_prev
        m_t = jnp.maximum(inter, jnp.max(dmat, axis=0, keepdims=True))
        a = (st * jnp.exp(dmat - m_t)).astype(BF16)
        w_inter = jnp.exp(inter - m_t)
        num = (jnp.dot(vth, a, preferred_element_type=F32)
               + w_inter * jnp.dot(c_prev.astype(BF16), qtm, preferred_element_type=F32))
        den = jnp.maximum(jnp.abs(num[ML_DIM:ML_DIM + 1]), jnp.exp(-m_t))
        outs.append(num[:ML_DIM] * (1.0 / den))
        b_end = b_row[:, last:last + 1]
        g_row = b_end - b_row + li_row
        m_new = jnp.maximum(b_end + m_prev, jnp.max(g_row, axis=1, keepdims=True))
        wk = jnp.exp(g_row - m_new)
        decay = jnp.exp(b_end + m_prev - m_new)
        vw = (vth.astype(F32) * wk).astype(BF16)
        c_ref[d, h] = decay * c_prev + jnp.dot(vw, k, preferred_element_type=F32)
        m_ref[sidx:sidx + 1, :] = jnp.broadcast_to(m_new, (1, LANES))
    ht_ref[0] = jnp.concatenate(outs, axis=0)


def _mlstm_kernel(kf_ref, qf_ref, vf_ref, gf_ref, gtf_ref, kb_ref, qb_ref, vb_ref, gb_ref, gtb_ref,
                  hf_ref, hb_ref, c_ref, m_ref):
    @pl.when(pl.program_id(1) == 0)
    def _():
        c_ref[...] = jnp.zeros_like(c_ref)
        m_ref[...] = jnp.zeros_like(m_ref)

    _mlstm_direction(kf_ref[0], qf_ref[0], vf_ref[0], gf_ref[0], gtf_ref[0], c_ref, m_ref, 0, hf_ref)
    _mlstm_direction(kb_ref[0], qb_ref[0], vb_ref[0], gb_ref[0], gtb_ref[0], c_ref, m_ref, 1, hb_ref)


def _mlstm(ck, cqt, cvt, gates, gates_t):
    B, S, gw = ck.shape
    nc = S // TM

    def fwd_chunk(i):
        return i

    def bwd_chunk(i):
        return jnp.where(i == 0, 0, nc - i)

    def specs(chunk):
        def tok(width):
            return pl.BlockSpec((1, TM, width), lambda b, i: (b, chunk(i), 0))

        def tok_t(rows):
            return pl.BlockSpec((1, rows, TM), lambda b, i: (b, 0, chunk(i)))

        return [tok(gw), tok_t(gw), tok_t(cvt.shape[1]), tok(LANES), tok_t(gates_t.shape[1])], tok_t(gw)

    in_f, out_f = specs(fwd_chunk)
    in_b, out_b = specs(bwd_chunk)
    return pl.pallas_call(
        _mlstm_kernel,
        grid=(B, nc),
        in_specs=in_f + in_b,
        out_specs=[out_f, out_b],
        out_shape=[jax.ShapeDtypeStruct((B, gw, S), F32)] * 2,
        scratch_shapes=[pltpu.VMEM((2, ML_HEADS, VT_ROWS, gw), F32),
                        pltpu.VMEM((2 * ML_HEADS, LANES), F32)],
        compiler_params=_cparams(("parallel", "arbitrary")),
        name="mlstm_scan",
    )(ck, cqt, cvt, gates, gates_t, ck, cqt, cvt, gates, gates_t)


def _pool_mixer(u_ref, up_ref, un_ref, ext_ref, pw_ref, ps_ref, first_tile):
    j = pl.program_id(1) + first_tile
    nt = pl.num_programs(1) + first_tile
    has_prev = j >= 2
    has_next = jnp.logical_and(j >= 1, j < nt - 1)
    u = u_ref[0]
    ext_ref[0:POOL_HALO, :] = jnp.where(has_prev, up_ref[0], 0.0)
    ext_ref[POOL_HALO:POOL_HALO + TM, :] = u
    ext_ref[POOL_HALO + TM:, :] = jnp.where(has_next, un_ref[0], 0.0)

    def shifted(s):
        return ext_ref[POOL_HALO + s:POOL_HALO + s + TM, :]

    lane = lax.broadcasted_iota(jnp.int32, (TM, GROUP_W), 1)
    r = lax.broadcasted_iota(jnp.int32, (TM, GROUP_W), 0)
    far = 2 * POOL_HALO
    left_room = r + jnp.where(has_prev, far, 0)
    right_room = (TM - 1 - r) + jnp.where(has_next, far, 0)
    total = u
    mean = jnp.zeros_like(u)
    prev_half = 0
    for gi, w in enumerate(POOL_WINDOWS):
        half = w // 2
        for s in list(range(-half, -prev_half)) + list(range(max(prev_half, 1), half)):
            total = total + shifted(s)
        prev_half = half
        cnt = jnp.minimum(left_room, half) + jnp.minimum(right_room, half - 1) + 1
        mean = jnp.where(lane // POOL_GROUP == gi, total / cnt.astype(F32), mean)
    dlt = (mean - u).astype(BF16)
    return jnp.dot(dlt, pw_ref[...], preferred_element_type=F32) * ps_ref[...]


def _out_kernel(x_ref, mod_ref, a_ref, u_ref, up_ref, un_ref, hf_ref, hb_ref, co_ref, d_ref,
                w_ref, pw_ref, ps_ref, mg_ref, gm_ref, g_ref, b_ref, o_ref, ext_ref, *, first_tile):
    gw = GROUP_W
    x = x_ref[0]
    mod = mod_ref[0]
    b_mix = _pool_mixer(u_ref, up_ref, un_ref, ext_ref, pw_ref, ps_ref, first_tile)
    hm = (hf_ref[0] + hb_ref[0]).T
    c_mix = (hm * lax.rsqrt(_group_mean(hm * hm, gm_ref[...]) + EPS) * mg_ref[...]
             * jax.nn.sigmoid(co_ref[0].astype(F32)))
    o = (jnp.dot(a_ref[0], w_ref[0:gw, :], preferred_element_type=F32)
         + jnp.dot(b_mix.astype(BF16), w_ref[gw:2 * gw, :], preferred_element_type=F32)
         + jnp.dot(c_mix.astype(BF16), w_ref[2 * gw:3 * gw, :], preferred_element_type=F32)
         + jnp.dot(d_ref[0], w_ref[3 * gw:4 * gw, :], preferred_element_type=F32))
    y = ALPHA * x + mod[5:6] * o
    o_ref[0] = _layer_norm(y) * g_ref[...] + b_ref[...]


def _out_proj(xs, mod, a, u, hf, hb, co, dd, w_out, pool_bd, pool_scale, ml_g, gm, g, b, keep_ctx):
    B, S, _ = xs.shape
    gw = GROUP_W
    hb_per_tile = TM // POOL_HALO
    n_halo = S // POOL_HALO
    t0 = 0 if keep_ctx else 1

    def tok(width):
        return pl.BlockSpec((1, TM, width), lambda b_, j: (b_, j + t0, 0))

    def tok_t(rows):
        return pl.BlockSpec((1, rows, TM), lambda b_, j: (b_, 0, j + t0))

    prev_spec = pl.BlockSpec((1, POOL_HALO, gw),
                             lambda b_, j: (b_, jnp.maximum((j + t0) * hb_per_tile - 1, 0), 0))
    next_spec = pl.BlockSpec((1, POOL_HALO, gw),
                             lambda b_, j: (b_, jnp.minimum((j + t0 + 1) * hb_per_tile, n_halo - 1), 0))
    return pl.pallas_call(
        functools.partial(_out_kernel, first_tile=t0),
        grid=(B, S // TM - t0),
        in_specs=[tok(D_MODEL), _mod_spec(keep_ctx), tok(gw), tok(gw), prev_spec, next_spec,
                  tok_t(gw), tok_t(gw), tok(gw), tok(gw),
                  _const_spec((D_MODEL, D_MODEL)), _const_spec((gw, gw)), _const_spec((1, gw)),
                  _const_spec((1, gw)), _const_spec((gw, gw)),
                  _const_spec((1, D_MODEL)), _const_spec((1, D_MODEL))],
        out_specs=pl.BlockSpec((1, TM, D_MODEL), lambda b_, j: (b_, j, 0)),
        out_shape=jax.ShapeDtypeStruct((B, S - t0 * TM, D_MODEL), F32),
        scratch_shapes=[pltpu.VMEM((TM + 2 * POOL_HALO, gw), F32)],
        compiler_params=_cparams(("parallel", "parallel")),
        name="mix_out_proj",
    )(xs, mod, a, u, u, u, hf, hb, co, dd, w_out, pool_bd, pool_scale, ml_g, gm, g, b)


def _rope_tables(L, Lc, dim):
    rows = L // GRID_W
    row = jnp.repeat(jnp.arange(rows), GRID_W).astype(F32)
    col = jnp.tile(jnp.arange(GRID_W), rows).astype(F32)
    axis_dim = dim // 2
    inv = ROPE_THETA ** (-jnp.arange(0, axis_dim, 2, dtype=F32) / axis_dim)
    ang = jnp.concatenate([row[:, None] * inv, col[:, None] * inv], axis=-1)
    cos, sin = jnp.cos(ang), jnp.sin(ang)
    reps = LANES // dim
    cos_p = jnp.tile(jnp.concatenate([cos, cos], axis=-1), (1, reps))
    sin_p = jnp.tile(jnp.concatenate([-sin, sin], axis=-1), (1, reps))
    cos_p = jnp.concatenate([jnp.ones((Lc, LANES), F32), cos_p], axis=0)
    sin_p = jnp.concatenate([jnp.zeros((Lc, LANES), F32), sin_p], axis=0)
    return cos_p, sin_p


def _reorder_w_in(w):
    g0 = 8 * GROUP_W
    g1 = g0 + 4 * ML_HEADS
    pad = jnp.zeros((w.shape[0], LANES - 4 * ML_HEADS), w.dtype)
    return jnp.concatenate([w[:, :g0], w[:, g1:], w[:, g0:g1], pad], axis=1).astype(BF16)


def _block_diag(blocks):
    n = blocks.shape[0]
    rows = []
    for i in range(n):
        rows.append(jnp.concatenate([blocks[i] if k == i else jnp.zeros_like(blocks[i])
                                     for k in range(n)], axis=1))
    return jnp.concatenate(rows, axis=0)


def kernel(x, c, ctx, c_ctx, w_ada, b_ada, ln_g, ln_b, ffn1_wi, ffn1_wo, ffn2_wi, ffn2_wo, w_in, w_out,
           diff_lambda, diff_norm_g, pool_w, pool_scale, ml_gate_b, ml_norm_g, gqa_qnorm_g, gqa_knorm_g):
    B, L, _ = x.shape
    Lc = ctx.shape[1]
    assert Lc == TM and L % KV_BLOCK == 0 and L % GRID_W == 0 and B + 1 <= MOD_ROWS
    depth = w_ada.shape[0]

    cc = jnp.concatenate([c_ctx[None], c, jnp.zeros((MOD_ROWS - 1 - B, D_MODEL), F32)], axis=0)
    mod_all = _modulation(cc, w_ada, b_ada).reshape(depth, MOD_ROWS, N_MOD, D_MODEL)

    rope = _rope_tables(L, Lc, DA_QK) + _rope_tables(L, Lc, GQA_DIM)
    gm = _block_diag(jnp.full((GROUP_W // HEAD_W, HEAD_W, HEAD_W), 1.0 / HEAD_W, BF16))

    xs = (ctx, x)
    for l in range(depth):
        last = l == depth - 1
        mod = mod_all[l]
        lam_init = 0.8 - 0.6 * math.exp(-0.3 * l)
        xs = _ffn(xs, mod, ffn1_wi[l].astype(BF16), ffn1_wo[l].astype(BF16), ln_g[l, 0], ln_b[l, 0], 0)
        gate_b = jnp.concatenate([ml_gate_b[l].reshape(1, -1),
                                  jnp.zeros((1, LANES - 4 * ML_HEADS), F32)], axis=1)
        qg = jnp.tile(gqa_qnorm_g[l], GQA_HEADS).reshape(1, -1)
        kg = jnp.tile(gqa_knorm_g[l], GQA_KV_HEADS).reshape(1, -1)
        (qat, ka, vat, u, cqt, ck, cvt, co, gates, gates_t, qdt, kd, vdt) = _in_proj(
            xs, mod, _reorder_w_in(w_in[l]), rope, qg, kg, gate_b, gm)
        a = _attention(functools.partial(_diff_attn_kernel, lam_init=lam_init), qat, ka, vat,
                       [diff_lambda[l], diff_norm_g[l].reshape(1, -1)], 2 * DA_HEADS, "diff_attention")
        dd = _attention(_gqa_kernel, qdt, kd, vdt, [], GQA_HEADS, "gqa_attention")
        hf, hb = _mlstm(ck, cqt, cvt, gates, gates_t)
        xs = _out_proj(xs, mod, a, u, hf, hb, co, dd, w_out[l].astype(BF16),
                       _block_diag(pool_w[l]).astype(BF16), pool_scale[l].reshape(1, -1),
                       ml_norm_g[l].reshape(1, -1), gm, ln_g[l, 1].reshape(1, -1),
                       ln_b[l, 1].reshape(1, -1), keep_ctx=not last)
        xs = _ffn(xs, mod, ffn2_wi[l].astype(BF16), ffn2_wo[l].astype(BF16), ln_g[l, 2], ln_b[l, 2], 2,
                  has_ctx=not last)
    return xs
```

```python
import functools
import math

import jax
import jax.numpy as jnp
from jax import lax
from jax.experimental import pallas as pl
from jax.experimental.pallas import tpu as pltpu

F32 = jnp.float32
BF16 = jnp.bfloat16

D_MODEL = 1024
DEPTH = 2
GRID_W = 64
GROUP_W = 256
D_FF = 2816
N_MOD = 9
EPS = 1e-6
ROPE_THETA = 10000.0
DA_HEADS = 4
DA_QK = 32
DA_V = 64
POOL_WINDOWS = (2, 4, 8, 16)
POOL_GROUP = 64
POOL_HALO = 8
ML_HEADS = 4
ML_DIM = 64
ML_GATE_COLS = 4 * ML_HEADS
GQA_HEADS = 4
GQA_KV_HEADS = 2
GQA_DIM = 64
HEAD_W = 64
LANES = 128
BF16_SUBLANES = 16
VT_ROWS = HEAD_W + BF16_SUBLANES
ALPHA = (2.0 * DEPTH) ** 0.25
LOG2E = math.log2(math.e)

TM = 256
KV_BLOCK = 512
ATTN_UNROLL = 4
FF_CHUNKS = ((0, 1024), (1024, 1024), (2048, 768))
IN_W = 2688
MOD_ROWS = 16
MOD_TN = 1152
VMEM_LIMIT = 56 * 1024 * 1024


def _cparams(sem):
    return pltpu.CompilerParams(dimension_semantics=sem, vmem_limit_bytes=VMEM_LIMIT)


def _const_spec(shape):
    nd = len(shape)
    return pl.BlockSpec(shape, lambda b, j: (0,) * nd, pipeline_mode=pl.Buffered(1))


def _mod_spec(has_ctx=True):
    if has_ctx:
        return pl.BlockSpec((1, N_MOD, D_MODEL), lambda b, j: (jnp.where(j == 0, 0, b + 1), 0, 0))
    return pl.BlockSpec((1, N_MOD, D_MODEL), lambda b, j: (b + 1, 0, 0))


def _layer_norm(x):
    mu = jnp.mean(x, axis=-1, keepdims=True)
    xc = x - mu
    var = jnp.mean(xc * xc, axis=-1, keepdims=True)
    return xc * lax.rsqrt(var + EPS)


def _group_mean(x, gm):
    hi = x.astype(BF16)
    lo = (x - hi.astype(F32)).astype(BF16)
    return (jnp.dot(hi, gm, preferred_element_type=F32) + jnp.dot(lo, gm, preferred_element_type=F32))


def _mod_kernel(c_ref, w_ref, b_ref, o_ref):
    c = c_ref[...]
    s = c * jax.nn.sigmoid(c)
    o_ref[0] = jnp.dot(s, w_ref[0], preferred_element_type=F32,
                       precision=lax.Precision.HIGHEST) + b_ref[0]


def _modulation(cc, w_ada, b_ada):
    depth = w_ada.shape[0]
    n = N_MOD * D_MODEL
    return pl.pallas_call(
        _mod_kernel,
        grid=(depth, n // MOD_TN),
        in_specs=[pl.BlockSpec((MOD_ROWS, D_MODEL), lambda l, j: (0, 0)),
                  pl.BlockSpec((1, D_MODEL, MOD_TN), lambda l, j: (l, 0, j)),
                  pl.BlockSpec((1, 1, MOD_TN), lambda l, j: (l, 0, j))],
        out_specs=pl.BlockSpec((1, MOD_ROWS, MOD_TN), lambda l, j: (l, 0, j)),
        out_shape=jax.ShapeDtypeStruct((depth, MOD_ROWS, n), F32),
        compiler_params=_cparams(("parallel", "parallel")),
        name="adaln_mod",
    )(cc, w_ada, b_ada.reshape(depth, 1, n))


def _ffn_kernel(*refs, s, split_input, tiles_per_sample):
    n_data = 4 if split_input else 1
    data = refs[:n_data]
    mod0_ref, mod1_ref, wi_ref, wo_ref, g_ref, b_ref, o_ref = refs[n_data:]
    xs = []
    for half in range(2):
        if split_input:
            j = (2 * pl.program_id(0) + half) % tiles_per_sample
            xs.append(jnp.where(j == 0, data[2 * half][0], data[2 * half + 1][0]))
        else:
            xs.append(data[0][half])
    mods = [mod0_ref[0], mod1_ref[0]]
    hs = [(_layer_norm(x) * (1.0 + mod[3 * s + 1:3 * s + 2]) + mod[3 * s:3 * s + 1]).astype(BF16)
          for x, mod in zip(xs, mods)]
    accs = []
    for h in hs:
        acc = jnp.zeros((TM, D_MODEL), F32)
        for c0, cw in FF_CHUNKS:
            gate = jnp.dot(h, wi_ref[:, c0:c0 + cw], preferred_element_type=F32)
            up = jnp.dot(h, wi_ref[:, D_FF + c0:D_FF + c0 + cw], preferred_element_type=F32)
            act = (gate * jax.nn.sigmoid(gate) * up).astype(BF16)
            acc = acc + jnp.dot(act, wo_ref[c0:c0 + cw, :], preferred_element_type=F32)
        accs.append(acc)
    for half, (x, mod, acc) in enumerate(zip(xs, mods, accs)):
        y = ALPHA * x + (0.5 * mod[3 * s + 2:3 * s + 3]) * acc
        o_ref[half] = _layer_norm(y) * g_ref[...] + b_ref[...]


def _ffn(xs, mod, wi, wo, g, b, s, has_ctx=True):
    split_input = isinstance(xs, tuple)
    if split_input:
        ctx, x = xs
        B, S = x.shape[0], ctx.shape[1] + x.shape[1]
    else:
        B, S, _ = xs.shape
    n = S // TM
    assert (B * n) % 2 == 0

    def sample_tile(p, half):
        sub = 2 * p + half
        return sub // n, sub % n

    def mod_spec(half):
        def index(p):
            b_, j = sample_tile(p, half)
            return (jnp.where(jnp.logical_and(has_ctx, j == 0), 0, b_ + 1), 0, 0)
        return pl.BlockSpec((1, N_MOD, D_MODEL), index)

    def const(shape):
        return pl.BlockSpec(shape, lambda p: (0,) * len(shape), pipeline_mode=pl.Buffered(1))

    if split_input:
        data, data_specs = [], []
        for half in range(2):
            data += [ctx, x]
            data_specs += [
                pl.BlockSpec((1, TM, D_MODEL), lambda p, h=half: (sample_tile(p, h)[0], 0, 0)),
                pl.BlockSpec((1, TM, D_MODEL),
                             lambda p, h=half: (sample_tile(p, h)[0],
                                                jnp.maximum(sample_tile(p, h)[1] - 1, 0), 0))]
    else:
        data = [xs.reshape(B * n, TM, D_MODEL)]
        data_specs = [pl.BlockSpec((2, TM, D_MODEL), lambda p: (p, 0, 0))]
    out = pl.pallas_call(
        functools.partial(_ffn_kernel, s=s, split_input=split_input, tiles_per_sample=n),
        grid=(B * n // 2,),
        in_specs=data_specs + [mod_spec(0), mod_spec(1), const((D_MODEL, 2 * D_FF)),
                               const((D_FF, D_MODEL)), const((1, D_MODEL)), const((1, D_MODEL))],
        out_specs=pl.BlockSpec((2, TM, D_MODEL), lambda p: (p, 0, 0)),
        out_shape=jax.ShapeDtypeStruct((B * n, TM, D_MODEL), F32),
        compiler_params=_cparams(("parallel",)),
        name="macaron_ffn",
    )(*data, mod, mod, wi, wo, g.reshape(1, D_MODEL), b.reshape(1, D_MODEL))
    return out.reshape(B, S, D_MODEL)


def _rope(x, cos, sin_signed, half):
    w = x.shape[1]
    reps = w // LANES
    if reps > 1:
        cos = jnp.concatenate([cos] * reps, axis=1)
        sin_signed = jnp.concatenate([sin_signed] * reps, axis=1)
    lane = lax.broadcasted_iota(jnp.int32, x.shape, 1)
    first = (lane % (2 * half)) < half
    partner = jnp.where(first, pltpu.roll(x, w - half, 1), pltpu.roll(x, half, 1))
    return x * cos + partner * sin_signed


def _log_sigmoid(x):
    return jnp.minimum(x, 0.0) - jnp.log1p(jnp.exp(-jnp.abs(x)))


def _values_t_with_ones(v, n_heads):
    vt = v.T
    ones = jnp.ones((VT_ROWS - HEAD_W, v.shape[0]), v.dtype)
    parts = []
    for h in range(n_heads):
        parts += [vt[h * HEAD_W:(h + 1) * HEAD_W], ones]
    return jnp.concatenate(parts, axis=0)


def _mlstm_gate_slab(graw):
    t = graw.shape[0]
    lane = lax.broadcasted_iota(jnp.int32, graw.shape, 1)
    is_forget = ((lane // ML_HEADS) % 2) == 1
    gl = jnp.where(is_forget, _log_sigmoid(graw), graw)
    row = lax.broadcasted_iota(jnp.int32, (t, t), 0)
    col = lax.broadcasted_iota(jnp.int32, (t, t), 1)
    tri = jnp.where(col <= row, 1.0, 0.0).astype(BF16)
    hi = gl.astype(BF16)
    r1 = gl - hi.astype(F32)
    mid = r1.astype(BF16)
    lo = (r1 - mid.astype(F32)).astype(BF16)
    csum = (jnp.dot(tri, hi, preferred_element_type=F32) + jnp.dot(tri, mid, preferred_element_type=F32)
            + jnp.dot(tri, lo, preferred_element_type=F32))
    rsum = csum[t - 1:t] - csum + gl
    cum = jnp.where(lane >= 2 * ML_HEADS, rsum, csum)
    return jnp.where(lane < ML_GATE_COLS, gl, pltpu.roll(cum, ML_GATE_COLS, 1))


def _in_kernel(x_ref, mod_ref, w_ref, cos_a, sin_a, cos_d, sin_d, qg_ref, kg_ref, gb_ref, gm_ref,
               qat_ref, ka_ref, vat_ref, u_ref, cqt_ref, ck_ref, cvt_ref, co_ref, g_ref, gt_ref,
               qdt_ref, kd_ref, vdt_ref):
    x = x_ref[0]
    mod = mod_ref[0]
    h = (_layer_norm(x) * (1.0 + mod[4:5]) + mod[3:4]).astype(BF16)
    y = jnp.dot(h, w_ref[...], preferred_element_type=F32)
    gw = GROUP_W
    ca, sa = cos_a[...], sin_a[...]
    qa = _rope(y[:, 0:gw], ca, sa, DA_QK // 2) * (DA_QK ** -0.5 * LOG2E)
    qat_ref[0] = qa.T.astype(BF16)
    ka_ref[0] = _rope(y[:, gw:2 * gw], ca, sa, DA_QK // 2).astype(BF16)
    vat_ref[0] = _values_t_with_ones(y[:, 2 * gw:3 * gw], DA_HEADS).astype(BF16)
    u_ref[0] = y[:, 3 * gw:4 * gw]
    cqt_ref[0] = (y[:, 4 * gw:5 * gw] * ML_DIM ** -0.5).T.astype(BF16)
    ck_ref[0] = y[:, 5 * gw:6 * gw].astype(BF16)
    cvt_ref[0] = _values_t_with_ones(y[:, 6 * gw:7 * gw], ML_HEADS).astype(BF16)
    co_ref[0] = y[:, 7 * gw:8 * gw].astype(BF16)
    gates = _mlstm_gate_slab(y[:, 10 * gw:10 * gw + LANES] + gb_ref[...])
    g_ref[0] = gates
    gt_ref[0] = gates.T[0:2 * ML_GATE_COLS]
    gm = gm_ref[...]
    cd, sd = cos_d[...], sin_d[...]
    qd = y[:, 8 * gw:9 * gw]
    qd = qd * lax.rsqrt(_group_mean(qd * qd, gm) + EPS) * qg_ref[...]
    qdt_ref[0] = (_rope(qd, cd, sd, GQA_DIM // 2) * (GQA_DIM ** -0.5 * LOG2E)).T.astype(BF16)
    kvw = GQA_KV_HEADS * GQA_DIM
    kd = y[:, 9 * gw:9 * gw + kvw]
    kd = kd * lax.rsqrt(_group_mean(kd * kd, gm[:kvw, :kvw]) + EPS) * kg_ref[...]
    kd_ref[0] = _rope(kd, cd, sd, GQA_DIM // 2).astype(BF16)
    vdt_ref[0] = _values_t_with_ones(y[:, 9 * gw + kvw:9 * gw + 2 * kvw], GQA_KV_HEADS).astype(BF16)


def _in_proj(xs, mod, w_in, rope, qg, kg, gate_b, gm):
    B, S, _ = xs.shape
    gw = GROUP_W
    kvw = GQA_KV_HEADS * GQA_DIM

    def tok(width):
        return pl.BlockSpec((1, TM, width), lambda b, j: (b, j, 0))

    def tok_t(rows):
        return pl.BlockSpec((1, rows, TM), lambda b, j: (b, 0, j))

    def table():
        return pl.BlockSpec((TM, LANES), lambda b, j: (j, 0))

    out_shapes = [
        ((B, gw, S), BF16, tok_t(gw)),
        ((B, S, gw), BF16, tok(gw)),
        ((B, DA_HEADS * VT_ROWS, S), BF16, tok_t(DA_HEADS * VT_ROWS)),
        ((B, S, gw), F32, tok(gw)),
        ((B, gw, S), BF16, tok_t(gw)),
        ((B, S, gw), BF16, tok(gw)),
        ((B, ML_HEADS * VT_ROWS, S), BF16, tok_t(ML_HEADS * VT_ROWS)),
        ((B, S, gw), BF16, tok(gw)),
        ((B, S, LANES), F32, tok(LANES)),
        ((B, 2 * ML_GATE_COLS, S), F32, tok_t(2 * ML_GATE_COLS)),
        ((B, gw, S), BF16, tok_t(gw)),
        ((B, S, kvw), BF16, tok(kvw)),
        ((B, GQA_KV_HEADS * VT_ROWS, S), BF16, tok_t(GQA_KV_HEADS * VT_ROWS)),
    ]
    return pl.pallas_call(
        _in_kernel,
        grid=(B, S // TM),
        in_specs=[tok(D_MODEL), _mod_spec(), _const_spec((D_MODEL, IN_W)),
                  table(), table(), table(), table(),
                  _const_spec((1, gw)), _const_spec((1, kvw)), _const_spec((1, LANES)),
                  _const_spec((gw, gw))],
        out_specs=[o[2] for o in out_shapes],
        out_shape=[jax.ShapeDtypeStruct(o[0], o[1]) for o in out_shapes],
        compiler_params=_cparams(("parallel", "parallel")),
        name="mix_in_proj",
    )(xs, mod, w_in, rope[0], rope[1], rope[2], rope[3], qg, kg, gate_b, gm)


def _flash_absorb(st, m_blk, vt, acc_ref, m_ref, c, first):
    if first:
        m_new = m_blk
    else:
        m_old = m_ref[c]
        m_new = jnp.maximum(m_old, m_blk)
    p = jnp.exp2((st - m_new).astype(BF16))
    pv = jnp.dot(vt, p, preferred_element_type=F32)
    if first:
        acc_ref[c] = pv
    else:
        acc_ref[c] = jnp.exp2(m_old - m_new) * acc_ref[c] + pv
    m_ref[c] = m_new


def _flash_sweep(k_ref, vt_ref, k_lane_of_chain, v_row_of_chain, qw_ref, acc_ref, m_ref, st_ref,
                 mb_ref):
    n_chains = len(v_row_of_chain)
    n_blk = (k_ref.shape[1] - TM) // KV_BLOCK

    def scores(rows, c):
        l0 = k_lane_of_chain[c]
        return jnp.dot(k_ref[0, rows, l0:l0 + LANES], qw_ref[c], preferred_element_type=F32)

    def vt_blk(c, start, size):
        r0 = v_row_of_chain[c]
        return vt_ref[0, r0:r0 + VT_ROWS, pl.ds(start, size)]

    def latent_start(i):
        start = TM + i * KV_BLOCK
        return start if isinstance(i, int) else pl.multiple_of(start, TM)

    def ctx_scores():
        return [scores(slice(0, TM), c) for c in range(n_chains)]

    def absorb_ctx(sts):
        for c in range(n_chains):
            _flash_absorb(sts[c], jnp.max(sts[c], axis=0, keepdims=True), vt_blk(c, 0, TM), acc_ref,
                          m_ref, c, True)

    def produce_scores(rows, slot, c):
        st = scores(rows, c)
        st_ref[slot, c] = st
        mb_ref[slot, c] = jnp.max(st, axis=0, keepdims=True)

    def run_half_step(i, parity):
        nxt = min(i, n_blk - 1) if isinstance(i, int) else jnp.minimum(i, n_blk - 1)
        rows = pl.ds(latent_start(nxt), KV_BLOCK)
        for c in range(n_chains):
            produce_scores(rows, parity, c)
            _flash_absorb(st_ref[1 - parity, c], mb_ref[1 - parity, c],
                          vt_blk(c, latent_start(i - 1), KV_BLOCK), acc_ref, m_ref, c, False)

    @pl.when(pl.program_id(1) == 0)
    def _():
        absorb_ctx(ctx_scores())

    @pl.when(pl.program_id(1) > 0)
    def _():
        sts = ctx_scores()
        for c in range(n_chains):
            produce_scores(pl.ds(latent_start(0), KV_BLOCK), 0, c)
        absorb_ctx(sts)

        def body(t, _):
            for u in range(ATTN_UNROLL):
                run_half_step(t * ATTN_UNROLL + u + 1, (u + 1) % 2)
            return 0

        n_full = n_blk // ATTN_UNROLL
        lax.fori_loop(0, n_full, body, 0)
        for u in range(n_blk % ATTN_UNROLL):
            run_half_step(n_full * ATTN_UNROLL + u + 1, (u + 1) % 2)


def _attn_out_t(acc):
    return acc[:HEAD_W] * (1.0 / acc[HEAD_W:HEAD_W + 1])


def _diff_attn_kernel(qt_ref, k_ref, vt_ref, lam_ref, g_ref, o_ref, qw_ref, acc_ref, m_ref, st_ref,
                      mb_ref, *, lam_init):
    n_chains = 2 * DA_HEADS
    per_slab = LANES // DA_QK
    feat = lax.broadcasted_iota(jnp.int32, (LANES, TM), 0)
    for c in range(n_chains):
        slab = c // per_slab
        qt = qt_ref[0, slab * LANES:(slab + 1) * LANES, :]
        qw_ref[c] = jnp.where(feat // DA_QK == c % per_slab, qt, jnp.zeros_like(qt))
    _flash_sweep(k_ref, vt_ref, [LANES * (c // per_slab) for c in range(n_chains)],
                 [VT_ROWS * (c // 2) for c in range(n_chains)], qw_ref, acc_ref, m_ref,
                 st_ref, mb_ref)
    dl = lam_ref[...]
    lam = (jnp.exp(jnp.sum(dl[0:1] * dl[1:2], axis=1, keepdims=True))
           - jnp.exp(jnp.sum(dl[2:3] * dl[3:4], axis=1, keepdims=True)) + lam_init)
    outs = []
    for h in range(DA_HEADS):
        o = _attn_out_t(acc_ref[2 * h]) - lam * _attn_out_t(acc_ref[2 * h + 1])
        outs.append(o * lax.rsqrt(jnp.mean(o * o, axis=0, keepdims=True) + EPS))
    o_nat = jnp.concatenate(outs, axis=0).T
    o_ref[0] = (o_nat * g_ref[...] * (1.0 - lam_init)).astype(BF16)


def _gqa_kernel(qt_ref, k_ref, vt_ref, o_ref, qw_ref, acc_ref, m_ref, st_ref, mb_ref):
    qt = qt_ref[0]
    group = GQA_HEADS // GQA_KV_HEADS
    zeros = jnp.zeros((GQA_DIM, qt.shape[1]), qt.dtype)
    for qh in range(GQA_HEADS):
        q_rows = qt[qh * GQA_DIM:(qh + 1) * GQA_DIM]
        kvh = qh // group
        qw_ref[qh] = jnp.concatenate([q_rows if i == kvh else zeros for i in range(GQA_KV_HEADS)],
                                     axis=0)
    _flash_sweep(k_ref, vt_ref, [0] * GQA_HEADS, [VT_ROWS * (qh // group) for qh in range(GQA_HEADS)],
                 qw_ref, acc_ref, m_ref, st_ref, mb_ref)
    o_t = jnp.concatenate([_attn_out_t(acc_ref[qh]) for qh in range(GQA_HEADS)], axis=0)
    o_ref[0] = o_t.T.astype(BF16)


def _attention(kernel_fn, qt, k, vt, extra, n_chains, name):
    B, S, kw = k.shape
    gw = qt.shape[1]
    extra_specs = [_const_spec(e.shape) for e in extra]
    return pl.pallas_call(
        kernel_fn,
        grid=(B, S // TM),
        in_specs=[pl.BlockSpec((1, gw, TM), lambda b, j: (b, 0, j)),
                  pl.BlockSpec((1, S, kw), lambda b, j: (b, 0, 0)),
                  pl.BlockSpec((1,) + vt.shape[1:], lambda b, j: (b, 0, 0))] + extra_specs,
        out_specs=pl.BlockSpec((1, TM, gw), lambda b, j: (b, j, 0)),
        out_shape=jax.ShapeDtypeStruct((B, S, gw), BF16),
        scratch_shapes=[pltpu.VMEM((n_chains, LANES, TM), BF16),
                        pltpu.VMEM((n_chains, VT_ROWS, TM), F32),
                        pltpu.VMEM((n_chains, 1, TM), F32),
                        pltpu.VMEM((2, n_chains, KV_BLOCK, TM), F32),
                        pltpu.VMEM((2, n_chains, 1, TM), F32)],
        compiler_params=_cparams(("parallel", "arbitrary")),
        name=name,
    )(qt, k, vt, *extra)


def _mlstm_direction(k, qt, vt, g, gt, c_ref, m_ref, d, ht_ref):
    T = k.shape[0]
    key_pos = lax.broadcasted_iota(jnp.int32, (T, T), 0)
    qry_pos = lax.broadcasted_iota(jnp.int32, (T, T), 1)
    seen = (key_pos <= qry_pos) if d == 0 else (key_pos >= qry_pos)
    feat = lax.broadcasted_iota(jnp.int32, qt.shape, 0)
    last = T - 1 if d == 0 else 0
    outs = []
    for h in range(ML_HEADS):
        li_c = 2 * ML_HEADS * d + h
        b_c = ML_GATE_COLS + li_c + ML_HEADS
        sidx = d * ML_HEADS + h
        qtm = jnp.where(feat // ML_DIM == h, qt, jnp.zeros_like(qt))
        r_col = g[:, li_c:li_c + 1] - g[:, b_c:b_c + 1]
        b_row = gt[b_c:b_c + 1, :]
        li_row = gt[li_c:li_c + 1, :]
        m_prev = m_ref[sidx:sidx + 1, 0:1]
        c_prev = c_ref[d, h]
        vth = vt[VT_ROWS * h:VT_ROWS * (h + 1)]
        st = jnp.dot(k, qtm, preferred_element_type=F32)
        dmat = jnp.where(seen, b_row + r_col, -jnp.inf)
        inter = b_row + m_prev
        m_t = jnp.maximum(inter, jnp.max(dmat, axis=0, keepdims=True))
        a = (st * jnp.exp(dmat - m_t)).astype(BF16)
        w_inter = jnp.exp(inter - m_t)
        num = (jnp.dot(vth, a, preferred_element_type=F32)
               + w_inter * jnp.dot(c_prev.astype(BF16), qtm, preferred_element_type=F32))
        den = jnp.maximum(jnp.abs(num[ML_DIM:ML_DIM + 1]), jnp.exp(-m_t))
        outs.append(num[:ML_DIM] * (1.0 / den))
        b_end = b_row[:, last:last + 1]
        g_row = b_end - b_row + li_row
        m_new = jnp.maximum(b_end + m_prev, jnp.max(g_row, axis=1, keepdims=True))
        wk = jnp.exp(g_row - m_new)
        decay = jnp.exp(b_end + m_prev - m_new)
        vw = (vth.astype(F32) * wk).astype(BF16)
        c_ref[d, h] = decay * c_prev + jnp.dot(vw, k, preferred_element_type=F32)
        m_ref[sidx:sidx + 1, :] = jnp.broadcast_to(m_new, (1, LANES))
    ht_ref[0] = jnp.concatenate(outs, axis=0)


def _mlstm_kernel(kf_ref, qf_ref, vf_ref, gf_ref, gtf_ref, kb_ref, qb_ref, vb_ref, gb_ref, gtb_ref,
                  hf_ref, hb_ref, c_ref, m_ref):
    @pl.when(pl.program_id(1) == 0)
    def _():
        c_ref[...] = jnp.zeros_like(c_ref)
        m_ref[...] = jnp.zeros_like(m_ref)

    _mlstm_direction(kf_ref[0], qf_ref[0], vf_ref[0], gf_ref[0], gtf_ref[0], c_ref, m_ref, 0, hf_ref)
    _mlstm_direction(kb_ref[0], qb_ref[0], vb_ref[0], gb_ref[0], gtb_ref[0], c_ref, m_ref, 1, hb_ref)


def _mlstm(ck, cqt, cvt, gates, gates_t):
    B, S, gw = ck.shape
    nc = S // TM

    def fwd_chunk(i):
        return i

    def bwd_chunk(i):
        return jnp.where(i == 0, 0, nc - i)

    def specs(chunk):
        def tok(width):
            return pl.BlockSpec((1, TM, width), lambda b, i: (b, chunk(i), 0))

        def tok_t(rows):
            return pl.BlockSpec((1, rows, TM), lambda b, i: (b, 0, chunk(i)))

        return [tok(gw), tok_t(gw), tok_t(cvt.shape[1]), tok(LANES), tok_t(gates_t.shape[1])], tok_t(gw)

    in_f, out_f = specs(fwd_chunk)
    in_b, out_b = specs(bwd_chunk)
    return pl.pallas_call(
        _mlstm_kernel,
        grid=(B, nc),
        in_specs=in_f + in_b,
        out_specs=[out_f, out_b],
        out_shape=[jax.ShapeDtypeStruct((B, gw, S), F32)] * 2,
        scratch_shapes=[pltpu.VMEM((2, ML_HEADS, VT_ROWS, gw), F32),
                        pltpu.VMEM((2 * ML_HEADS, LANES), F32)],
        compiler_params=_cparams(("parallel", "arbitrary")),
        name="mlstm_scan",
    )(ck, cqt, cvt, gates, gates_t, ck, cqt, cvt, gates, gates_t)


def _pool_mixer(u_ref, up_ref, un_ref, ext_ref, pw_ref, ps_ref, first_tile):
    j = pl.program_id(1) + first_tile
    nt = pl.num_programs(1) + first_tile
    has_prev = j >= 2
    has_next = jnp.logical_and(j >= 1, j < nt - 1)
    u = u_ref[0]
    ext_ref[0:POOL_HALO, :] = jnp.where(has_prev, up_ref[0], 0.0)
    ext_ref[POOL_HALO:POOL_HALO + TM, :] = u
    ext_ref[POOL_HALO + TM:, :] = jnp.where(has_next, un_ref[0], 0.0)

    def shifted(s):
        return ext_ref[POOL_HALO + s:POOL_HALO + s + TM, :]

    lane = lax.broadcasted_iota(jnp.int32, (TM, GROUP_W), 1)
    r = lax.broadcasted_iota(jnp.int32, (TM, GROUP_W), 0)
    far = 2 * POOL_HALO
    left_room = r + jnp.where(has_prev, far, 0)
    right_room = (TM - 1 - r) + jnp.where(has_next, far, 0)
    total = u
    mean = jnp.zeros_like(u)
    prev_half = 0
    for gi, w in enumerate(POOL_WINDOWS):
        half = w // 2
        for s in list(range(-half, -prev_half)) + list(range(max(prev_half, 1), half)):
            total = total + shifted(s)
        prev_half = half
        cnt = jnp.minimum(left_room, half) + jnp.minimum(right_room, half - 1) + 1
        mean = jnp.where(lane // POOL_GROUP == gi, total / cnt.astype(F32), mean)
    dlt = (mean - u).astype(BF16)
    return jnp.dot(dlt, pw_ref[...], preferred_element_type=F32) * ps_ref[...]


def _out_kernel(x_ref, mod_ref, a_ref, u_ref, up_ref, un_ref, hf_ref, hb_ref, co_ref, d_ref,
                w_ref, pw_ref, ps_ref, mg_ref, gm_ref, g_ref, b_ref, o_ref, ext_ref, *, first_tile):
    gw = GROUP_W
    x = x_ref[0]
    mod = mod_ref[0]
    b_mix = _pool_mixer(u_ref, up_ref, un_ref, ext_ref, pw_ref, ps_ref, first_tile)
    hm = (hf_ref[0] + hb_ref[0]).T
    c_mix = (hm * lax.rsqrt(_group_mean(hm * hm, gm_ref[...]) + EPS) * mg_ref[...]
             * jax.nn.sigmoid(co_ref[0].astype(F32)))
    o = (jnp.dot(a_ref[0], w_ref[0:gw, :], preferred_element_type=F32)
         + jnp.dot(b_mix.astype(BF16), w_ref[gw:2 * gw, :], preferred_element_type=F32)
         + jnp.dot(c_mix.astype(BF16), w_ref[2 * gw:3 * gw, :], preferred_element_type=F32)
         + jnp.dot(d_ref[0], w_ref[3 * gw:4 * gw, :], preferred_element_type=F32))
    y = ALPHA * x + mod[5:6] * o
    o_ref[0] = _layer_norm(y) * g_ref[...] + b_ref[...]


def _out_proj(xs, mod, a, u, hf, hb, co, dd, w_out, pool_bd, pool_scale, ml_g, gm, g, b, keep_ctx):
    B, S, _ = xs.shape
    gw = GROUP_W
    hb_per_tile = TM // POOL_HALO
    n_halo = S // POOL_HALO
    t0 = 0 if keep_ctx else 1

    def tok(width):
        return pl.BlockSpec((1, TM, width), lambda b_, j: (b_, j + t0, 0))

    def tok_t(rows):
        return pl.BlockSpec((1, rows, TM), lambda b_, j: (b_, 0, j + t0))

    prev_spec = pl.BlockSpec((1, POOL_HALO, gw),
                             lambda b_, j: (b_, jnp.maximum((j + t0) * hb_per_tile - 1, 0), 0))
    next_spec = pl.BlockSpec((1, POOL_HALO, gw),
                             lambda b_, j: (b_, jnp.minimum((j + t0 + 1) * hb_per_tile, n_halo - 1), 0))
    return pl.pallas_call(
        functools.partial(_out_kernel, first_tile=t0),
        grid=(B, S // TM - t0),
        in_specs=[tok(D_MODEL), _mod_spec(keep_ctx), tok(gw), tok(gw), prev_spec, next_spec,
                  tok_t(gw), tok_t(gw), tok(gw), tok(gw),
                  _const_spec((D_MODEL, D_MODEL)), _const_spec((gw, gw)), _const_spec((1, gw)),
                  _const_spec((1, gw)), _const_spec((gw, gw)),
                  _const_spec((1, D_MODEL)), _const_spec((1, D_MODEL))],
        out_specs=pl.BlockSpec((1, TM, D_MODEL), lambda b_, j: (b_, j, 0)),
        out_shape=jax.ShapeDtypeStruct((B, S - t0 * TM, D_MODEL), F32),
        scratch_shapes=[pltpu.VMEM((TM + 2 * POOL_HALO, gw), F32)],
        compiler_params=_cparams(("parallel", "parallel")),
        name="mix_out_proj",
    )(xs, mod, a, u, u, u, hf, hb, co, dd, w_out, pool_bd, pool_scale, ml_g, gm, g, b)


def _rope_tables(L, Lc, dim):
    rows = L // GRID_W
    row = jnp.repeat(jnp.arange(rows), GRID_W).astype(F32)
    col = jnp.tile(jnp.arange(GRID_W), rows).astype(F32)
    axis_dim = dim // 2
    inv = ROPE_THETA ** (-jnp.arange(0, axis_dim, 2, dtype=F32) / axis_dim)
    ang = jnp.concatenate([row[:, None] * inv, col[:, None] * inv], axis=-1)
    cos, sin = jnp.cos(ang), jnp.sin(ang)
    reps = LANES // dim
    cos_p = jnp.tile(jnp.concatenate([cos, cos], axis=-1), (1, reps))
    sin_p = jnp.tile(jnp.concatenate([-sin, sin], axis=-1), (1, reps))
    cos_p = jnp.concatenate([jnp.ones((Lc, LANES), F32), cos_p], axis=0)
    sin_p = jnp.concatenate([jnp.zeros((Lc, LANES), F32), sin_p], axis=0)
    return cos_p, sin_p


def _reorder_w_in(w):
    g0 = 8 * GROUP_W
    g1 = g0 + 4 * ML_HEADS
    pad = jnp.zeros((w.shape[0], LANES - 4 * ML_HEADS), w.dtype)
    return jnp.concatenate([w[:, :g0], w[:, g1:], w[:, g0:g1], pad], axis=1).astype(BF16)


def _block_diag(blocks):
    n = blocks.shape[0]
    rows = []
    for i in range(n):
        rows.append(jnp.concatenate([blocks[i] if k == i else jnp.zeros_like(blocks[i])
                                     for k in range(n)], axis=1))
    return jnp.concatenate(rows, axis=0)


def kernel(x, c, ctx, c_ctx, w_ada, b_ada, ln_g, ln_b, ffn1_wi, ffn1_wo, ffn2_wi, ffn2_wo, w_in, w_out,
           diff_lambda, diff_norm_g, pool_w, pool_scale, ml_gate_b, ml_norm_g, gqa_qnorm_g, gqa_knorm_g):
    B, L, _ = x.shape
    Lc = ctx.shape[1]
    assert Lc == TM and L % KV_BLOCK == 0 and L % GRID_W == 0 and B + 1 <= MOD_ROWS
    depth = w_ada.shape[0]

    cc = jnp.concatenate([c_ctx[None], c, jnp.zeros((MOD_ROWS - 1 - B, D_MODEL), F32)], axis=0)
    mod_all = _modulation(cc, w_ada, b_ada).reshape(depth, MOD_ROWS, N_MOD, D_MODEL)

    rope = _rope_tables(L, Lc, DA_QK) + _rope_tables(L, Lc, GQA_DIM)
    gm = _block_diag(jnp.full((GROUP_W // HEAD_W, HEAD_W, HEAD_W), 1.0 / HEAD_W, BF16))

    xs = (ctx, x)
    for l in range(depth):
        last = l == depth - 1
        mod = mod_all[l]
        lam_init = 0.8 - 0.6 * math.exp(-0.3 * l)
        xs = _ffn(xs, mod, ffn1_wi[l].astype(BF16), ffn1_wo[l].astype(BF16), ln_g[l, 0], ln_b[l, 0], 0)
        gate_b = jnp.concatenate([ml_gate_b[l].reshape(1, -1),
                                  jnp.zeros((1, LANES - 4 * ML_HEADS), F32)], axis=1)
        qg = jnp.tile(gqa_qnorm_g[l], GQA_HEADS).reshape(1, -1)
        kg = jnp.tile(gqa_knorm_g[l], GQA_KV_HEADS).reshape(1, -1)
        (qat, ka, vat, u, cqt, ck, cvt, co, gates, gates_t, qdt, kd, vdt) = _in_proj(
            xs, mod, _reorder_w_in(w_in[l]), rope, qg, kg, gate_b, gm)
        a = _attention(functools.partial(_diff_attn_kernel, lam_init=lam_init), qat, ka, vat,
                       [diff_lambda[l], diff_norm_g[l].reshape(1, -1)], 2 * DA_HEADS, "diff_attention")
        dd = _attention(_gqa_kernel, qdt, kd, vdt, [], GQA_HEADS, "gqa_attention")
        hf, hb = _mlstm(ck, cqt, cvt, gates, gates_t)
        xs = _out_proj(xs, mod, a, u, hf, hb, co, dd, w_out[l].astype(BF16),
                       _block_diag(pool_w[l]).astype(BF16), pool_scale[l].reshape(1, -1),
                       ml_norm_g[l].reshape(1, -1), gm, ln_g[l, 1].reshape(1, -1),
                       ln_b[l, 1].reshape(1, -1), keep_ctx=not last)
        xs = _ffn(xs, mod, ffn2_wi[l].astype(BF16), ffn2_wo[l].astype(BF16), ln_g[l, 2], ln_b[l, 2], 2,
                  has_ctx=not last)
    return xs
```

```python
import functools
import math

import jax
import jax.numpy as jnp
from jax import lax
from jax.experimental import pallas as pl
from jax.experimental.pallas import tpu as pltpu

F32 = jnp.float32
BF16 = jnp.bfloat16

D_MODEL = 1024
DEPTH = 2
GRID_W = 64
GROUP_W = 256
D_FF = 2816
N_MOD = 9
EPS = 1e-6
ROPE_THETA = 10000.0
DA_HEADS = 4
DA_QK = 32
DA_V = 64
POOL_WINDOWS = (2, 4, 8, 16)
POOL_GROUP = 64
POOL_HALO = 8
ML_HEADS = 4
ML_DIM = 64
ML_GATE_COLS = 4 * ML_HEADS
GQA_HEADS = 4
GQA_KV_HEADS = 2
GQA_DIM = 64
HEAD_W = 64
LANES = 128
BF16_SUBLANES = 16
VT_ROWS = HEAD_W + BF16_SUBLANES
ALPHA = (2.0 * DEPTH) ** 0.25
LOG2E = math.log2(math.e)

TM = 256
KV_BLOCK = 256
ATTN_UNROLL = 16
FF_CHUNKS = ((0, 1024), (1024, 1024), (2048, 768))
IN_W = 2688
MOD_ROWS = 16
MOD_TN = 1152
VMEM_LIMIT = 56 * 1024 * 1024


def _cparams(sem):
    return pltpu.CompilerParams(dimension_semantics=sem, vmem_limit_bytes=VMEM_LIMIT)


def _const_spec(shape):
    nd = len(shape)
    return pl.BlockSpec(shape, lambda b, j: (0,) * nd, pipeline_mode=pl.Buffered(1))


def _mod_spec(has_ctx=True):
    if has_ctx:
        return pl.BlockSpec((1, N_MOD, D_MODEL), lambda b, j: (jnp.where(j == 0, 0, b + 1), 0, 0))
    return pl.BlockSpec((1, N_MOD, D_MODEL), lambda b, j: (b + 1, 0, 0))


def _layer_norm(x):
    mu = jnp.mean(x, axis=-1, keepdims=True)
    xc = x - mu
    var = jnp.mean(xc * xc, axis=-1, keepdims=True)
    return xc * lax.rsqrt(var + EPS)


def _group_mean(x, gm):
    hi = x.astype(BF16)
    lo = (x - hi.astype(F32)).astype(BF16)
    return (jnp.dot(hi, gm, preferred_element_type=F32) + jnp.dot(lo, gm, preferred_element_type=F32))


def _mod_kernel(c_ref, w_ref, b_ref, o_ref):
    c = c_ref[...]
    s = c * jax.nn.sigmoid(c)
    o_ref[0] = jnp.dot(s, w_ref[0], preferred_element_type=F32,
                       precision=lax.Precision.HIGHEST) + b_ref[0]


def _modulation(cc, w_ada, b_ada):
    depth = w_ada.shape[0]
    n = N_MOD * D_MODEL
    return pl.pallas_call(
        _mod_kernel,
        grid=(depth, n // MOD_TN),
        in_specs=[pl.BlockSpec((MOD_ROWS, D_MODEL), lambda l, j: (0, 0)),
                  pl.BlockSpec((1, D_MODEL, MOD_TN), lambda l, j: (l, 0, j)),
                  pl.BlockSpec((1, 1, MOD_TN), lambda l, j: (l, 0, j))],
        out_specs=pl.BlockSpec((1, MOD_ROWS, MOD_TN), lambda l, j: (l, 0, j)),
        out_shape=jax.ShapeDtypeStruct((depth, MOD_ROWS, n), F32),
        compiler_params=_cparams(("parallel", "parallel")),
        name="adaln_mod",
    )(cc, w_ada, b_ada.reshape(depth, 1, n))


def _ffn_kernel(*refs, s, split_input, tiles_per_sample):
    n_data = 4 if split_input else 1
    data = refs[:n_data]
    mod0_ref, mod1_ref, wi_ref, wo_ref, g_ref, b_ref, o_ref = refs[n_data:]
    xs = []
    for half in range(2):
        if split_input:
            j = (2 * pl.program_id(0) + half) % tiles_per_sample
            xs.append(jnp.where(j == 0, data[2 * half][0], data[2 * half + 1][0]))
        else:
            xs.append(data[0][half])
    mods = [mod0_ref[0], mod1_ref[0]]
    hs = [(_layer_norm(x) * (1.0 + mod[3 * s + 1:3 * s + 2]) + mod[3 * s:3 * s + 1]).astype(BF16)
          for x, mod in zip(xs, mods)]
    accs = []
    for h in hs:
        acc = jnp.zeros((TM, D_MODEL), F32)
        for c0, cw in FF_CHUNKS:
            gate = jnp.dot(h, wi_ref[:, c0:c0 + cw], preferred_element_type=F32)
            up = jnp.dot(h, wi_ref[:, D_FF + c0:D_FF + c0 + cw], preferred_element_type=F32)
            act = (gate * jax.nn.sigmoid(gate) * up).astype(BF16)
            acc = acc + jnp.dot(act, wo_ref[c0:c0 + cw, :], preferred_element_type=F32)
        accs.append(acc)
    for half, (x, mod, acc) in enumerate(zip(xs, mods, accs)):
        y = ALPHA * x + (0.5 * mod[3 * s + 2:3 * s + 3]) * acc
        o_ref[half] = _layer_norm(y) * g_ref[...] + b_ref[...]


def _ffn(xs, mod, wi, wo, g, b, s, has_ctx=True):
    split_input = isinstance(xs, tuple)
    if split_input:
        ctx, x = xs
        B, S = x.shape[0], ctx.shape[1] + x.shape[1]
    else:
        B, S, _ = xs.shape
    n = S // TM
    assert (B * n) % 2 == 0

    def sample_tile(p, half):
        sub = 2 * p + half
        return sub // n, sub % n

    def mod_spec(half):
        def index(p):
            b_, j = sample_tile(p, half)
            return (jnp.where(jnp.logical_and(has_ctx, j == 0), 0, b_ + 1), 0, 0)
        return pl.BlockSpec((1, N_MOD, D_MODEL), index)

    def const(shape):
        return pl.BlockSpec(shape, lambda p: (0,) * len(shape), pipeline_mode=pl.Buffered(1))

    if split_input:
        data, data_specs = [], []
        for half in range(2):
            data += [ctx, x]
            data_specs += [
                pl.BlockSpec((1, TM, D_MODEL), lambda p, h=half: (sample_tile(p, h)[0], 0, 0)),
                pl.BlockSpec((1, TM, D_MODEL),
                             lambda p, h=half: (sample_tile(p, h)[0],
                                                jnp.maximum(sample_tile(p, h)[1] - 1, 0), 0))]
    else:
        data = [xs.reshape(B * n, TM, D_MODEL)]
        data_specs = [pl.BlockSpec((2, TM, D_MODEL), lambda p: (p, 0, 0))]
    out = pl.pallas_call(
        functools.partial(_ffn_kernel, s=s, split_input=split_input, tiles_per_sample=n),
        grid=(B * n // 2,),
        in_specs=data_specs + [mod_spec(0), mod_spec(1), const((D_MODEL, 2 * D_FF)),
                               const((D_FF, D_MODEL)), const((1, D_MODEL)), const((1, D_MODEL))],
        out_specs=pl.BlockSpec((2, TM, D_MODEL), lambda p: (p, 0, 0)),
        out_shape=jax.ShapeDtypeStruct((B * n, TM, D_MODEL), F32),
        compiler_params=_cparams(("parallel",)),
        name="macaron_ffn",
    )(*data, mod, mod, wi, wo, g.reshape(1, D_MODEL), b.reshape(1, D_MODEL))
    return out.reshape(B, S, D_MODEL)


def _rope(x, cos, sin_signed, half):
    w = x.shape[1]
    reps = w // LANES
    if reps > 1:
        cos = jnp.concatenate([cos] * reps, axis=1)
        sin_signed = jnp.concatenate([sin_signed] * reps, axis=1)
    lane = lax.broadcasted_iota(jnp.int32, x.shape, 1)
    first = (lane % (2 * half)) < half
    partner = jnp.where(first, pltpu.roll(x, w - half, 1), pltpu.roll(x, half, 1))
    return x * cos + partner * sin_signed


def _log_sigmoid(x):
    return jnp.minimum(x, 0.0) - jnp.log1p(jnp.exp(-jnp.abs(x)))


def _values_t_with_ones(v, n_heads):
    vt = v.T
    ones = jnp.ones((VT_ROWS - HEAD_W, v.shape[0]), v.dtype)
    parts = []
    for h in range(n_heads):
        parts += [vt[h * HEAD_W:(h + 1) * HEAD_W], ones]
    return jnp.concatenate(parts, axis=0)


def _mlstm_gate_slab(graw):
    t = graw.shape[0]
    lane = lax.broadcasted_iota(jnp.int32, graw.shape, 1)
    is_forget = ((lane // ML_HEADS) % 2) == 1
    gl = jnp.where(is_forget, _log_sigmoid(graw), graw)
    row = lax.broadcasted_iota(jnp.int32, (t, t), 0)
    col = lax.broadcasted_iota(jnp.int32, (t, t), 1)
    tri = jnp.where(col <= row, 1.0, 0.0).astype(BF16)
    hi = gl.astype(BF16)
    r1 = gl - hi.astype(F32)
    mid = r1.astype(BF16)
    lo = (r1 - mid.astype(F32)).astype(BF16)
    csum = (jnp.dot(tri, hi, preferred_element_type=F32) + jnp.dot(tri, mid, preferred_element_type=F32)
            + jnp.dot(tri, lo, preferred_element_type=F32))
    rsum = csum[t - 1:t] - csum + gl
    cum = jnp.where(lane >= 2 * ML_HEADS, rsum, csum)
    return jnp.where(lane < ML_GATE_COLS, gl, pltpu.roll(cum, ML_GATE_COLS, 1))


def _in_kernel(x_ref, mod_ref, w_ref, cos_a, sin_a, cos_d, sin_d, qg_ref, kg_ref, gb_ref, gm_ref,
               qat_ref, ka_ref, vat_ref, u_ref, cqt_ref, ck_ref, cvt_ref, co_ref, g_ref, gt_ref,
               qdt_ref, kd_ref, vdt_ref):
    x = x_ref[0]
    mod = mod_ref[0]
    h = (_layer_norm(x) * (1.0 + mod[4:5]) + mod[3:4]).astype(BF16)
    y = jnp.dot(h, w_ref[...], preferred_element_type=F32)
    gw = GROUP_W
    ca, sa = cos_a[...], sin_a[...]
    qa = _rope(y[:, 0:gw], ca, sa, DA_QK // 2) * (DA_QK ** -0.5 * LOG2E)
    qat_ref[0] = qa.T.astype(BF16)
    ka_ref[0] = _rope(y[:, gw:2 * gw], ca, sa, DA_QK // 2).astype(BF16)
    vat_ref[0] = _values_t_with_ones(y[:, 2 * gw:3 * gw], DA_HEADS).astype(BF16)
    u_ref[0] = y[:, 3 * gw:4 * gw]
    cqt_ref[0] = (y[:, 4 * gw:5 * gw] * ML_DIM ** -0.5).T.astype(BF16)
    ck_ref[0] = y[:, 5 * gw:6 * gw].astype(BF16)
    cvt_ref[0] = _values_t_with_ones(y[:, 6 * gw:7 * gw], ML_HEADS).astype(BF16)
    co_ref[0] = y[:, 7 * gw:8 * gw].astype(BF16)
    gates = _mlstm_gate_slab(y[:, 10 * gw:10 * gw + LANES] + gb_ref[...])
    g_ref[0] = gates
    gt_ref[0] = gates.T[0:2 * ML_GATE_COLS]
    gm = gm_ref[...]
    cd, sd = cos_d[...], sin_d[...]
    qd = y[:, 8 * gw:9 * gw]
    qd = qd * lax.rsqrt(_group_mean(qd * qd, gm) + EPS) * qg_ref[...]
    qdt_ref[0] = (_rope(qd, cd, sd, GQA_DIM // 2) * (GQA_DIM ** -0.5 * LOG2E)).T.astype(BF16)
    kvw = GQA_KV_HEADS * GQA_DIM
    kd = y[:, 9 * gw:9 * gw + kvw]
    kd = kd * lax.rsqrt(_group_mean(kd * kd, gm[:kvw, :kvw]) + EPS) * kg_ref[...]
    kd_ref[0] = _rope(kd, cd, sd, GQA_DIM // 2).astype(BF16)
    vdt_ref[0] = _values_t_with_ones(y[:, 9 * gw + kvw:9 * gw + 2 * kvw], GQA_KV_HEADS).astype(BF16)


def _in_proj(xs, mod, w_in, rope, qg, kg, gate_b, gm):
    B, S, _ = xs.shape
    gw = GROUP_W
    kvw = GQA_KV_HEADS * GQA_DIM

    def tok(width):
        return pl.BlockSpec((1, TM, width), lambda b, j: (b, j, 0))

    def tok_t(rows):
        return pl.BlockSpec((1, rows, TM), lambda b, j: (b, 0, j))

    def table():
        return pl.BlockSpec((TM, LANES), lambda b, j: (j, 0))

    out_shapes = [
        ((B, gw, S), BF16, tok_t(gw)),
        ((B, S, gw), BF16, tok(gw)),
        ((B, DA_HEADS * VT_ROWS, S), BF16, tok_t(DA_HEADS * VT_ROWS)),
        ((B, S, gw), F32, tok(gw)),
        ((B, gw, S), BF16, tok_t(gw)),
        ((B, S, gw), BF16, tok(gw)),
        ((B, ML_HEADS * VT_ROWS, S), BF16, tok_t(ML_HEADS * VT_ROWS)),
        ((B, S, gw), BF16, tok(gw)),
        ((B, S, LANES), F32, tok(LANES)),
        ((B, 2 * ML_GATE_COLS, S), F32, tok_t(2 * ML_GATE_COLS)),
        ((B, gw, S), BF16, tok_t(gw)),
        ((B, S, kvw), BF16, tok(kvw)),
        ((B, GQA_KV_HEADS * VT_ROWS, S), BF16, tok_t(GQA_KV_HEADS * VT_ROWS)),
    ]
    return pl.pallas_call(
        _in_kernel,
        grid=(B, S // TM),
        in_specs=[tok(D_MODEL), _mod_spec(), _const_spec((D_MODEL, IN_W)),
                  table(), table(), table(), table(),
                  _const_spec((1, gw)), _const_spec((1, kvw)), _const_spec((1, LANES)),
                  _const_spec((gw, gw))],
        out_specs=[o[2] for o in out_shapes],
        out_shape=[jax.ShapeDtypeStruct(o[0], o[1]) for o in out_shapes],
        compiler_params=_cparams(("parallel", "parallel")),
        name="mix_in_proj",
    )(xs, mod, w_in, rope[0], rope[1], rope[2], rope[3], qg, kg, gate_b, gm)


def _flash_absorb(st, m_blk, vt, acc_ref, m_ref, c, first):
    if first:
        m_new = m_blk
    else:
        m_old = m_ref[c]
        m_new = jnp.maximum(m_old, m_blk)
    p = jnp.exp2((st - m_new).astype(BF16))
    pv = jnp.dot(vt, p, preferred_element_type=F32)
    if first:
        acc_ref[c] = pv
    else:
        acc_ref[c] = jnp.exp2(m_old - m_new) * acc_ref[c] + pv
    m_ref[c] = m_new


def _flash_sweep(k_ref, vt_ref, k_lane_of_chain, v_row_of_chain, qw_ref, acc_ref, m_ref, st_ref,
                 mb_ref):
    n_chains = len(v_row_of_chain)
    n_blk = (k_ref.shape[1] - TM) // KV_BLOCK

    def scores(rows, c):
        l0 = k_lane_of_chain[c]
        return jnp.dot(k_ref[0, rows, l0:l0 + LANES], qw_ref[c], preferred_element_type=F32)

    def vt_blk(c, start, size):
        r0 = v_row_of_chain[c]
        return vt_ref[0, r0:r0 + VT_ROWS, pl.ds(start, size)]

    def latent_start(i):
        start = TM + i * KV_BLOCK
        return start if isinstance(i, int) else pl.multiple_of(start, TM)

    def ctx_scores():
        return [scores(slice(0, TM), c) for c in range(n_chains)]

    def absorb_ctx(sts):
        for c in range(n_chains):
            _flash_absorb(sts[c], jnp.max(sts[c], axis=0, keepdims=True), vt_blk(c, 0, TM), acc_ref,
                          m_ref, c, True)

    def produce_scores(rows, slot, c):
        st = scores(rows, c)
        st_ref[slot, c] = st
        mb_ref[slot, c] = jnp.max(st, axis=0, keepdims=True)

    def run_half_step(i, parity):
        nxt = min(i, n_blk - 1) if isinstance(i, int) else jnp.minimum(i, n_blk - 1)
        rows = pl.ds(latent_start(nxt), KV_BLOCK)
        for c in range(n_chains):
            produce_scores(rows, parity, c)
            _flash_absorb(st_ref[1 - parity, c], mb_ref[1 - parity, c],
                          vt_blk(c, latent_start(i - 1), KV_BLOCK), acc_ref, m_ref, c, False)

    @pl.when(pl.program_id(1) == 0)
    def _():
        absorb_ctx(ctx_scores())

    @pl.when(pl.program_id(1) > 0)
    def _():
        sts = ctx_scores()
        for c in range(n_chains):
            produce_scores(pl.ds(latent_start(0), KV_BLOCK), 0, c)
        absorb_ctx(sts)

        def body(t, _):
            for u in range(ATTN_UNROLL):
                run_half_step(t * ATTN_UNROLL + u + 1, (u + 1) % 2)
            return 0

        n_full = n_blk // ATTN_UNROLL
        lax.fori_loop(0, n_full, body, 0)
        for u in range(n_blk % ATTN_UNROLL):
            run_half_step(n_full * ATTN_UNROLL + u + 1, (u + 1) % 2)


def _attn_out_t(acc):
    return acc[:HEAD_W] * (1.0 / acc[HEAD_W:HEAD_W + 1])


def _diff_attn_kernel(qt_ref, k_ref, vt_ref, lam_ref, g_ref, o_ref, qw_ref, acc_ref, m_ref, st_ref,
                      mb_ref, *, lam_init):
    n_chains = 2 * DA_HEADS
    per_slab = LANES // DA_QK
    feat = lax.broadcasted_iota(jnp.int32, (LANES, TM), 0)
    for c in range(n_chains):
        slab = c // per_slab
        qt = qt_ref[0, slab * LANES:(slab + 1) * LANES, :]
        qw_ref[c] = jnp.where(feat // DA_QK == c % per_slab, qt, jnp.zeros_like(qt))
    _flash_sweep(k_ref, vt_ref, [LANES * (c // per_slab) for c in range(n_chains)],
                 [VT_ROWS * (c // 2) for c in range(n_chains)], qw_ref, acc_ref, m_ref,
                 st_ref, mb_ref)
    dl = lam_ref[...]
    lam = (jnp.exp(jnp.sum(dl[0:1] * dl[1:2], axis=1, keepdims=True))
           - jnp.exp(jnp.sum(dl[2:3] * dl[3:4], axis=1, keepdims=True)) + lam_init)
    outs = []
    for h in range(DA_HEADS):
        o = _attn_out_t(acc_ref[2 * h]) - lam * _attn_out_t(acc_ref[2 * h + 1])
        outs.append(o * lax.rsqrt(jnp.mean(o * o, axis=0, keepdims=True) + EPS))
    o_nat = jnp.concatenate(outs, axis=0).T
    o_ref[0] = (o_nat * g_ref[...] * (1.0 - lam_init)).astype(BF16)


def _gqa_kernel(qt_ref, k_ref, vt_ref, o_ref, qw_ref, acc_ref, m_ref, st_ref, mb_ref):
    qt = qt_ref[0]
    group = GQA_HEADS // GQA_KV_HEADS
    zeros = jnp.zeros((GQA_DIM, qt.shape[1]), qt.dtype)
    for qh in range(GQA_HEADS):
        q_rows = qt[qh * GQA_DIM:(qh + 1) * GQA_DIM]
        kvh = qh // group
        qw_ref[qh] = jnp.concatenate([q_rows if i == kvh else zeros for i in range(GQA_KV_HEADS)],
                                     axis=0)
    _flash_sweep(k_ref, vt_ref, [0] * GQA_HEADS, [VT_ROWS * (qh // group) for qh in range(GQA_HEADS)],
                 qw_ref, acc_ref, m_ref, st_ref, mb_ref)
    o_t = jnp.concatenate([_attn_out_t(acc_ref[qh]) for qh in range(GQA_HEADS)], axis=0)
    o_ref[0] = o_t.T.astype(BF16)


def _attention(kernel_fn, qt, k, vt, extra, n_chains, name):
    B, S, kw = k.shape
    gw = qt.shape[1]
    extra_specs = [_const_spec(e.shape) for e in extra]
    return pl.pallas_call(
        kernel_fn,
        grid=(B, S // TM),
        in_specs=[pl.BlockSpec((1, gw, TM), lambda b, j: (b, 0, j)),
                  pl.BlockSpec((1, S, kw), lambda b, j: (b, 0, 0)),
                  pl.BlockSpec((1,) + vt.shape[1:], lambda b, j: (b, 0, 0))] + extra_specs,
        out_specs=pl.BlockSpec((1, TM, gw), lambda b, j: (b, j, 0)),
        out_shape=jax.ShapeDtypeStruct((B, S, gw), BF16),
        scratch_shapes=[pltpu.VMEM((n_chains, LANES, TM), BF16),
                        pltpu.VMEM((n_chains, VT_ROWS, TM), F32),
                        pltpu.VMEM((n_chains, 1, TM), F32),
                        pltpu.VMEM((2, n_chains, KV_BLOCK, TM), F32),
                        pltpu.VMEM((2, n_chains, 1, TM), F32)],
        compiler_params=_cparams(("parallel", "arbitrary")),
        name=name,
    )(qt, k, vt, *extra)


def _mlstm_direction(k, qt, vt, g, gt, c_ref, m_ref, d, ht_ref):
    T = k.shape[0]
    key_pos = lax.broadcasted_iota(jnp.int32, (T, T), 0)
    qry_pos = lax.broadcasted_iota(jnp.int32, (T, T), 1)
    seen = (key_pos <= qry_pos) if d == 0 else (key_pos >= qry_pos)
    feat = lax.broadcasted_iota(jnp.int32, qt.shape, 0)
    last = T - 1 if d == 0 else 0
    outs = []
    for h in range(ML_HEADS):
        li_c = 2 * ML_HEADS * d + h
        b_c = ML_GATE_COLS + li_c + ML_HEADS
        sidx = d * ML_HEADS + h
        qtm = jnp.where(feat // ML_DIM == h, qt, jnp.zeros_like(qt))
        r_col = g[:, li_c:li_c + 1] - g[:, b_c:b_c + 1]
        b_row = gt[b_c:b_c + 1, :]
        li_row = gt[li_c:li_c + 1, :]
        m_prev = m_ref[sidx:sidx + 1, 0:1]
        c_prev = c_ref[d, h]
        vth = vt[VT_ROWS * h:VT_ROWS * (h + 1)]
        st = jnp.dot(k, qtm, preferred_element_type=F32)
        dmat = jnp.where(seen, b_row + r_col, -jnp.inf)
        inter = b_row + m_prev
        m_t = jnp.maximum(inter, jnp.max(dmat, axis=0, keepdims=True))
        a = (st * jnp.exp(dmat - m_t)).astype(BF16)
        w_inter = jnp.exp(inter - m_t)
        num = (jnp.dot(vth, a, preferred_element_type=F32)
               + w_inter * jnp.dot(c_prev.astype(BF16), qtm, preferred_element_type=F32))
        den = jnp.maximum(jnp.abs(num[ML_DIM:ML_DIM + 1]), jnp.exp(-m_t))
        outs.append(num[:ML_DIM] * (1.0 / den))
        b_end = b_row[:, last:last + 1]
        g_row = b_end - b_row + li_row
        m_new = jnp.maximum(b_end + m_prev, jnp.max(g_row, axis=1, keepdims=True))
        wk = jnp.exp(g_row - m_new)
        decay = jnp.exp(b_end + m_prev - m_new)
        vw = (vth.astype(F32) * wk).astype(BF16)
        c_ref[d, h] = decay * c_prev + jnp.dot(vw, k, preferred_element_type=F32)
        m_ref[sidx:sidx + 1, :] = jnp.broadcast_to(m_new, (1, LANES))
    ht_ref[0] = jnp.concatenate(outs, axis=0)


def _mlstm_kernel(kf_ref, qf_ref, vf_ref, gf_ref, gtf_ref, kb_ref, qb_ref, vb_ref, gb_ref, gtb_ref,
                  hf_ref, hb_ref, c_ref, m_ref):
    @pl.when(pl.program_id(1) == 0)
    def _():
        c_ref[...] = jnp.zeros_like(c_ref)
        m_ref[...] = jnp.zeros_like(m_ref)

    _mlstm_direction(kf_ref[0], qf_ref[0], vf_ref[0], gf_ref[0], gtf_ref[0], c_ref, m_ref, 0, hf_ref)
    _mlstm_direction(kb_ref[0], qb_ref[0], vb_ref[0], gb_ref[0], gtb_ref[0], c_ref, m_ref, 1, hb_ref)


def _mlstm(ck, cqt, cvt, gates, gates_t):
    B, S, gw = ck.shape
    nc = S // TM

    def fwd_chunk(i):
        return i

    def bwd_chunk(i):
        return jnp.where(i == 0, 0, nc - i)

    def specs(chunk):
        def tok(width):
            return pl.BlockSpec((1, TM, width), lambda b, i: (b, chunk(i), 0))

        def tok_t(rows):
            return pl.BlockSpec((1, rows, TM), lambda b, i: (b, 0, chunk(i)))

        return [tok(gw), tok_t(gw), tok_t(cvt.shape[1]), tok(LANES), tok_t(gates_t.shape[1])], tok_t(gw)

    in_f, out_f = specs(fwd_chunk)
    in_b, out_b = specs(bwd_chunk)
    return pl.pallas_call(
        _mlstm_kernel,
        grid=(B, nc),
        in_specs=in_f + in_b,
        out_specs=[out_f, out_b],
        out_shape=[jax.ShapeDtypeStruct((B, gw, S), F32)] * 2,
        scratch_shapes=[pltpu.VMEM((2, ML_HEADS, VT_ROWS, gw), F32),
                        pltpu.VMEM((2 * ML_HEADS, LANES), F32)],
        compiler_params=_cparams(("parallel", "arbitrary")),
        name="mlstm_scan",
    )(ck, cqt, cvt, gates, gates_t, ck, cqt, cvt, gates, gates_t)


def _pool_mixer(u_ref, up_ref, un_ref, ext_ref, pw_ref, ps_ref, first_tile):
    j = pl.program_id(1) + first_tile
    nt = pl.num_programs(1) + first_tile
    has_prev = j >= 2
    has_next = jnp.logical_and(j >= 1, j < nt - 1)
    u = u_ref[0]
    ext_ref[0:POOL_HALO, :] = jnp.where(has_prev, up_ref[0], 0.0)
    ext_ref[POOL_HALO:POOL_HALO + TM, :] = u
    ext_ref[POOL_HALO + TM:, :] = jnp.where(has_next, un_ref[0], 0.0)

    def shifted(s):
        return ext_ref[POOL_HALO + s:POOL_HALO + s + TM, :]

    lane = lax.broadcasted_iota(jnp.int32, (TM, GROUP_W), 1)
    r = lax.broadcasted_iota(jnp.int32, (TM, GROUP_W), 0)
    far = 2 * POOL_HALO
    left_room = r + jnp.where(has_prev, far, 0)
    right_room = (TM - 1 - r) + jnp.where(has_next, far, 0)
    total = u
    mean = jnp.zeros_like(u)
    prev_half = 0
    for gi, w in enumerate(POOL_WINDOWS):
        half = w // 2
        for s in list(range(-half, -prev_half)) + list(range(max(prev_half, 1), half)):
            total = total + shifted(s)
        prev_half = half
        cnt = jnp.minimum(left_room, half) + jnp.minimum(right_room, half - 1) + 1
        mean = jnp.where(lane // POOL_GROUP == gi, total / cnt.astype(F32), mean)
    dlt = (mean - u).astype(BF16)
    return jnp.dot(dlt, pw_ref[...], preferred_element_type=F32) * ps_ref[...]


def _out_kernel(x_ref, mod_ref, a_ref, u_ref, up_ref, un_ref, hf_ref, hb_ref, co_ref, d_ref,
                w_ref, pw_ref, ps_ref, mg_ref, gm_ref, g_ref, b_ref, o_ref, ext_ref, *, first_tile):
    gw = GROUP_W
    x = x_ref[0]
    mod = mod_ref[0]
    b_mix = _pool_mixer(u_ref, up_ref, un_ref, ext_ref, pw_ref, ps_ref, first_tile)
    hm = (hf_ref[0] + hb_ref[0]).T
    c_mix = (hm * lax.rsqrt(_group_mean(hm * hm, gm_ref[...]) + EPS) * mg_ref[...]
             * jax.nn.sigmoid(co_ref[0].astype(F32)))
    o = (jnp.dot(a_ref[0], w_ref[0:gw, :], preferred_element_type=F32)
         + jnp.dot(b_mix.astype(BF16), w_ref[gw:2 * gw, :], preferred_element_type=F32)
         + jnp.dot(c_mix.astype(BF16), w_ref[2 * gw:3 * gw, :], preferred_element_type=F32)
         + jnp.dot(d_ref[0], w_ref[3 * gw:4 * gw, :], preferred_element_type=F32))
    y = ALPHA * x + mod[5:6] * o
    o_ref[0] = _layer_norm(y) * g_ref[...] + b_ref[...]


def _out_proj(xs, mod, a, u, hf, hb, co, dd, w_out, pool_bd, pool_scale, ml_g, gm, g, b, keep_ctx):
    B, S, _ = xs.shape
    gw = GROUP_W
    hb_per_tile = TM // POOL_HALO
    n_halo = S // POOL_HALO
    t0 = 0 if keep_ctx else 1

    def tok(width):
        return pl.BlockSpec((1, TM, width), lambda b_, j: (b_, j + t0, 0))

    def tok_t(rows):
        return pl.BlockSpec((1, rows, TM), lambda b_, j: (b_, 0, j + t0))

    prev_spec = pl.BlockSpec((1, POOL_HALO, gw),
                             lambda b_, j: (b_, jnp.maximum((j + t0) * hb_per_tile - 1, 0), 0))
    next_spec = pl.BlockSpec((1, POOL_HALO, gw),
                             lambda b_, j: (b_, jnp.minimum((j + t0 + 1) * hb_per_tile, n_halo - 1), 0))
    return pl.pallas_call(
        functools.partial(_out_kernel, first_tile=t0),
        grid=(B, S // TM - t0),
        in_specs=[tok(D_MODEL), _mod_spec(keep_ctx), tok(gw), tok(gw), prev_spec, next_spec,
                  tok_t(gw), tok_t(gw), tok(gw), tok(gw),
                  _const_spec((D_MODEL, D_MODEL)), _const_spec((gw, gw)), _const_spec((1, gw)),
                  _const_spec((1, gw)), _const_spec((gw, gw)),
                  _const_spec((1, D_MODEL)), _const_spec((1, D_MODEL))],
        out_specs=pl.BlockSpec((1, TM, D_MODEL), lambda b_, j: (b_, j, 0)),
        out_shape=jax.ShapeDtypeStruct((B, S - t0 * TM, D_MODEL), F32),
        scratch_shapes=[pltpu.VMEM((TM + 2 * POOL_HALO, gw), F32)],
        compiler_params=_cparams(("parallel", "parallel")),
        name="mix_out_proj",
    )(xs, mod, a, u, u, u, hf, hb, co, dd, w_out, pool_bd, pool_scale, ml_g, gm, g, b)


def _rope_tables(L, Lc, dim):
    rows = L // GRID_W
    row = jnp.repeat(jnp.arange(rows), GRID_W).astype(F32)
    col = jnp.tile(jnp.arange(GRID_W), rows).astype(F32)
    axis_dim = dim // 2
    inv = ROPE_THETA ** (-jnp.arange(0, axis_dim, 2, dtype=F32) / axis_dim)
    ang = jnp.concatenate([row[:, None] * inv, col[:, None] * inv], axis=-1)
    cos, sin = jnp.cos(ang), jnp.sin(ang)
    reps = LANES // dim
    cos_p = jnp.tile(jnp.concatenate([cos, cos], axis=-1), (1, reps))
    sin_p = jnp.tile(jnp.concatenate([-sin, sin], axis=-1), (1, reps))
    cos_p = jnp.concatenate([jnp.ones((Lc, LANES), F32), cos_p], axis=0)
    sin_p = jnp.concatenate([jnp.zeros((Lc, LANES), F32), sin_p], axis=0)
    return cos_p, sin_p


def _reorder_w_in(w):
    g0 = 8 * GROUP_W
    g1 = g0 + 4 * ML_HEADS
    pad = jnp.zeros((w.shape[0], LANES - 4 * ML_HEADS), w.dtype)
    return jnp.concatenate([w[:, :g0], w[:, g1:], w[:, g0:g1], pad], axis=1).astype(BF16)


def _block_diag(blocks):
    n = blocks.shape[0]
    rows = []
    for i in range(n):
        rows.append(jnp.concatenate([blocks[i] if k == i else jnp.zeros_like(blocks[i])
                                     for k in range(n)], axis=1))
    return jnp.concatenate(rows, axis=0)


def kernel(x, c, ctx, c_ctx, w_ada, b_ada, ln_g, ln_b, ffn1_wi, ffn1_wo, ffn2_wi, ffn2_wo, w_in, w_out,
           diff_lambda, diff_norm_g, pool_w, pool_scale, ml_gate_b, ml_norm_g, gqa_qnorm_g, gqa_knorm_g):
    B, L, _ = x.shape
    Lc = ctx.shape[1]
    assert Lc == TM and L % KV_BLOCK == 0 and L % GRID_W == 0 and B + 1 <= MOD_ROWS
    depth = w_ada.shape[0]

    cc = jnp.concatenate([c_ctx[None], c, jnp.zeros((MOD_ROWS - 1 - B, D_MODEL), F32)], axis=0)
    mod_all = _modulation(cc, w_ada, b_ada).reshape(depth, MOD_ROWS, N_MOD, D_MODEL)

    rope = _rope_tables(L, Lc, DA_QK) + _rope_tables(L, Lc, GQA_DIM)
    gm = _block_diag(jnp.full((GROUP_W // HEAD_W, HEAD_W, HEAD_W), 1.0 / HEAD_W, BF16))

    xs = (ctx, x)
    for l in range(depth):
        last = l == depth - 1
        mod = mod_all[l]
        lam_init = 0.8 - 0.6 * math.exp(-0.3 * l)
        xs = _ffn(xs, mod, ffn1_wi[l].astype(BF16), ffn1_wo[l].astype(BF16), ln_g[l, 0], ln_b[l, 0], 0)
        gate_b = jnp.concatenate([ml_gate_b[l].reshape(1, -1),
                                  jnp.zeros((1, LANES - 4 * ML_HEADS), F32)], axis=1)
        qg = jnp.tile(gqa_qnorm_g[l], GQA_HEADS).reshape(1, -1)
        kg = jnp.tile(gqa_knorm_g[l], GQA_KV_HEADS).reshape(1, -1)
        (qat, ka, vat, u, cqt, ck, cvt, co, gates, gates_t, qdt, kd, vdt) = _in_proj(
            xs, mod, _reorder_w_in(w_in[l]), rope, qg, kg, gate_b, gm)
        a = _attention(functools.partial(_diff_attn_kernel, lam_init=lam_init), qat, ka, vat,
                       [diff_lambda[l], diff_norm_g[l].reshape(1, -1)], 2 * DA_HEADS, "diff_attention")
        dd = _attention(_gqa_kernel, qdt, kd, vdt, [], GQA_HEADS, "gqa_attention")
        hf, hb = _mlstm(ck, cqt, cvt, gates, gates_t)
        xs = _out_proj(xs, mod, a, u, hf, hb, co, dd, w_out[l].astype(BF16),
                       _block_diag(pool_w[l]).astype(BF16), pool_scale[l].reshape(1, -1),
                       ml_norm_g[l].reshape(1, -1), gm, ln_g[l, 1].reshape(1, -1),
                       ln_b[l, 1].reshape(1, -1), keep_ctx=not last)
        xs = _ffn(xs, mod, ffn2_wi[l].astype(BF16), ffn2_wo[l].astype(BF16), ln_g[l, 2], ln_b[l, 2], 2,
                  has_ctx=not last)
    return xs
```

```python
import functools
import math

import jax
import jax.numpy as jnp
from jax import lax
from jax.experimental import pallas as pl
from jax.experimental.pallas import tpu as pltpu

F32 = jnp.float32
BF16 = jnp.bfloat16

D_MODEL = 1024
DEPTH = 2
GRID_W = 64
GROUP_W = 256
D_FF = 2816
N_MOD = 9
EPS = 1e-6
ROPE_THETA = 10000.0
DA_HEADS = 4
DA_QK = 32
DA_V = 64
POOL_WINDOWS = (2, 4, 8, 16)
POOL_GROUP = 64
POOL_HALO = 8
ML_HEADS = 4
ML_DIM = 64
ML_GATE_COLS = 4 * ML_HEADS
GQA_HEADS = 4
GQA_KV_HEADS = 2
GQA_DIM = 64
HEAD_W = 64
LANES = 128
BF16_SUBLANES = 16
VT_ROWS = HEAD_W + BF16_SUBLANES
ALPHA = (2.0 * DEPTH) ** 0.25
LOG2E = math.log2(math.e)

TM = 256
KV_BLOCK = 256
ATTN_UNROLL = 16
FF_CHUNKS = ((0, 1024), (1024, 1024), (2048, 768))
IN_W = 2688
MOD_ROWS = 16
MOD_TN = 1152
VMEM_LIMIT = 56 * 1024 * 1024


def _cparams(sem):
    return pltpu.CompilerParams(dimension_semantics=sem, vmem_limit_bytes=VMEM_LIMIT)


def _const_spec(shape):
    nd = len(shape)
    return pl.BlockSpec(shape, lambda b, j: (0,) * nd, pipeline_mode=pl.Buffered(1))


def _mod_spec(has_ctx=True):
    if has_ctx:
        return pl.BlockSpec((1, N_MOD, D_MODEL), lambda b, j: (jnp.where(j == 0, 0, b + 1), 0, 0))
    return pl.BlockSpec((1, N_MOD, D_MODEL), lambda b, j: (b + 1, 0, 0))


def _layer_norm(x):
    mu = jnp.mean(x, axis=-1, keepdims=True)
    xc = x - mu
    var = jnp.mean(xc * xc, axis=-1, keepdims=True)
    return xc * lax.rsqrt(var + EPS)


def _group_mean(x, gm):
    hi = x.astype(BF16)
    lo = (x - hi.astype(F32)).astype(BF16)
    return (jnp.dot(hi, gm, preferred_element_type=F32) + jnp.dot(lo, gm, preferred_element_type=F32))


def _mod_kernel(c_ref, w_ref, b_ref, o_ref):
    c = c_ref[...]
    s = c * jax.nn.sigmoid(c)
    o_ref[0] = jnp.dot(s, w_ref[0], preferred_element_type=F32,
                       precision=lax.Precision.HIGHEST) + b_ref[0]


def _modulation(cc, w_ada, b_ada):
    depth = w_ada.shape[0]
    n = N_MOD * D_MODEL
    return pl.pallas_call(
        _mod_kernel,
        grid=(depth, n // MOD_TN),
        in_specs=[pl.BlockSpec((MOD_ROWS, D_MODEL), lambda l, j: (0, 0)),
                  pl.BlockSpec((1, D_MODEL, MOD_TN), lambda l, j: (l, 0, j)),
                  pl.BlockSpec((1, 1, MOD_TN), lambda l, j: (l, 0, j))],
        out_specs=pl.BlockSpec((1, MOD_ROWS, MOD_TN), lambda l, j: (l, 0, j)),
        out_shape=jax.ShapeDtypeStruct((depth, MOD_ROWS, n), F32),
        compiler_params=_cparams(("parallel", "parallel")),
        name="adaln_mod",
    )(cc, w_ada, b_ada.reshape(depth, 1, n))


def _ffn_kernel(*refs, s, split_input, tiles_per_sample):
    n_data = 4 if split_input else 1
    data = refs[:n_data]
    mod0_ref, mod1_ref, wi_ref, wo_ref, g_ref, b_ref, o_ref = refs[n_data:]
    xs = []
    for half in range(2):
        if split_input:
            j = (2 * pl.program_id(0) + half) % tiles_per_sample
            xs.append(jnp.where(j == 0, data[2 * half][0], data[2 * half + 1][0]))
        else:
            xs.append(data[0][half])
    mods = [mod0_ref[0], mod1_ref[0]]
    hs = [(_layer_norm(x) * (1.0 + mod[3 * s + 1:3 * s + 2]) + mod[3 * s:3 * s + 1]).astype(BF16)
          for x, mod in zip(xs, mods)]
    accs = []
    for h in hs:
        acc = jnp.zeros((TM, D_MODEL), F32)
        for c0, cw in FF_CHUNKS:
            gate = jnp.dot(h, wi_ref[:, c0:c0 + cw], preferred_element_type=F32)
            up = jnp.dot(h, wi_ref[:, D_FF + c0:D_FF + c0 + cw], preferred_element_type=F32)
            act = (gate * jax.nn.sigmoid(gate) * up).astype(BF16)
            acc = acc + jnp.dot(act, wo_ref[c0:c0 + cw, :], preferred_element_type=F32)
        accs.append(acc)
    for half, (x, mod, acc) in enumerate(zip(xs, mods, accs)):
        y = ALPHA * x + (0.5 * mod[3 * s + 2:3 * s + 3]) * acc
        o_ref[half] = _layer_norm(y) * g_ref[...] + b_ref[...]


def _ffn(xs, mod, wi, wo, g, b, s, has_ctx=True):
    split_input = isinstance(xs, tuple)
    if split_input:
        ctx, x = xs
        B, S = x.shape[0], ctx.shape[1] + x.shape[1]
    else:
        B, S, _ = xs.shape
    n = S // TM
    assert (B * n) % 2 == 0

    def sample_tile(p, half):
        sub = 2 * p + half
        return sub // n, sub % n

    def mod_spec(half):
        def index(p):
            b_, j = sample_tile(p, half)
            return (jnp.where(jnp.logical_and(has_ctx, j == 0), 0, b_ + 1), 0, 0)
        return pl.BlockSpec((1, N_MOD, D_MODEL), index)

    def const(shape):
        return pl.BlockSpec(shape, lambda p: (0,) * len(shape), pipeline_mode=pl.Buffered(1))

    if split_input:
        data, data_specs = [], []
        for half in range(2):
            data += [ctx, x]
            data_specs += [
                pl.BlockSpec((1, TM, D_MODEL), lambda p, h=half: (sample_tile(p, h)[0], 0, 0)),
                pl.BlockSpec((1, TM, D_MODEL),
                             lambda p, h=half: (sample_tile(p, h)[0],
                                                jnp.maximum(sample_tile(p, h)[1] - 1, 0), 0))]
    else:
        data = [xs.reshape(B * n, TM, D_MODEL)]
        data_specs = [pl.BlockSpec((2, TM, D_MODEL), lambda p: (p, 0, 0))]
    out = pl.pallas_call(
        functools.partial(_ffn_kernel, s=s, split_input=split_input, tiles_per_sample=n),
        grid=(B * n // 2,),
        in_specs=data_specs + [mod_spec(0), mod_spec(1), const((D_MODEL, 2 * D_FF)),
                               const((D_FF, D_MODEL)), const((1, D_MODEL)), const((1, D_MODEL))],
        out_specs=pl.BlockSpec((2, TM, D_MODEL), lambda p: (p, 0, 0)),
        out_shape=jax.ShapeDtypeStruct((B * n, TM, D_MODEL), F32),
        compiler_params=_cparams(("parallel",)),
        name="macaron_ffn",
    )(*data, mod, mod, wi, wo, g.reshape(1, D_MODEL), b.reshape(1, D_MODEL))
    return out.reshape(B, S, D_MODEL)


def _rope(x, cos, sin_signed, half):
    w = x.shape[1]
    reps = w // LANES
    if reps > 1:
        cos = jnp.concatenate([cos] * reps, axis=1)
        sin_signed = jnp.concatenate([sin_signed] * reps, axis=1)
    lane = lax.broadcasted_iota(jnp.int32, x.shape, 1)
    first = (lane % (2 * half)) < half
    partner = jnp.where(first, pltpu.roll(x, w - half, 1), pltpu.roll(x, half, 1))
    return x * cos + partner * sin_signed


def _log_sigmoid(x):
    return jnp.minimum(x, 0.0) - jnp.log1p(jnp.exp(-jnp.abs(x)))


def _values_t_with_ones(v, n_heads):
    vt = v.T
    ones = jnp.ones((VT_ROWS - HEAD_W, v.shape[0]), v.dtype)
    parts = []
    for h in range(n_heads):
        parts += [vt[h * HEAD_W:(h + 1) * HEAD_W], ones]
    return jnp.concatenate(parts, axis=0)


def _mlstm_gate_slab(graw):
    t = graw.shape[0]
    lane = lax.broadcasted_iota(jnp.int32, graw.shape, 1)
    is_forget = ((lane // ML_HEADS) % 2) == 1
    gl = jnp.where(is_forget, _log_sigmoid(graw), graw)
    row = lax.broadcasted_iota(jnp.int32, (t, t), 0)
    col = lax.broadcasted_iota(jnp.int32, (t, t), 1)
    tri = jnp.where(col <= row, 1.0, 0.0).astype(BF16)
    hi = gl.astype(BF16)
    r1 = gl - hi.astype(F32)
    mid = r1.astype(BF16)
    lo = (r1 - mid.astype(F32)).astype(BF16)
    csum = (jnp.dot(tri, hi, preferred_element_type=F32) + jnp.dot(tri, mid, preferred_element_type=F32)
            + jnp.dot(tri, lo, preferred_element_type=F32))
    rsum = csum[t - 1:t] - csum + gl
    cum = jnp.where(lane >= 2 * ML_HEADS, rsum, csum)
    return jnp.where(lane < ML_GATE_COLS, gl, pltpu.roll(cum, ML_GATE_COLS, 1))


def _in_kernel(x_ref, mod_ref, w_ref, cos_a, sin_a, cos_d, sin_d, qg_ref, kg_ref, gb_ref, gm_ref,
               qat_ref, ka_ref, vat_ref, u_ref, cqt_ref, ck_ref, cvt_ref, co_ref, g_ref, gt_ref,
               qdt_ref, kd_ref, vdt_ref):
    x = x_ref[0]
    mod = mod_ref[0]
    h = (_layer_norm(x) * (1.0 + mod[4:5]) + mod[3:4]).astype(BF16)
    y = jnp.dot(h, w_ref[...], preferred_element_type=F32)
    gw = GROUP_W
    ca, sa = cos_a[...], sin_a[...]
    qa = _rope(y[:, 0:gw], ca, sa, DA_QK // 2) * (DA_QK ** -0.5 * LOG2E)
    qat_ref[0] = qa.T.astype(BF16)
    ka_ref[0] = _rope(y[:, gw:2 * gw], ca, sa, DA_QK // 2).astype(BF16)
    vat_ref[0] = _values_t_with_ones(y[:, 2 * gw:3 * gw], DA_HEADS).astype(BF16)
    u_ref[0] = y[:, 3 * gw:4 * gw]
    cqt_ref[0] = (y[:, 4 * gw:5 * gw] * ML_DIM ** -0.5).T.astype(BF16)
    ck_ref[0] = y[:, 5 * gw:6 * gw].astype(BF16)
    cvt_ref[0] = _values_t_with_ones(y[:, 6 * gw:7 * gw], ML_HEADS).astype(BF16)
    co_ref[0] = y[:, 7 * gw:8 * gw].astype(BF16)
    gates = _mlstm_gate_slab(y[:, 10 * gw:10 * gw + LANES] + gb_ref[...])
    g_ref[0] = gates
    gt_ref[0] = gates.T[0:2 * ML_GATE_COLS]
    gm = gm_ref[...]
    cd, sd = cos_d[...], sin_d[...]
    qd = y[:, 8 * gw:9 * gw]
    qd = qd * lax.rsqrt(_group_mean(qd * qd, gm) + EPS) * qg_ref[...]
    qdt_ref[0] = (_rope(qd, cd, sd, GQA_DIM // 2) * (GQA_DIM ** -0.5 * LOG2E)).T.astype(BF16)
    kvw = GQA_KV_HEADS * GQA_DIM
    kd = y[:, 9 * gw:9 * gw + kvw]
    kd = kd * lax.rsqrt(_group_mean(kd * kd, gm[:kvw, :kvw]) + EPS) * kg_ref[...]
    kd_ref[0] = _rope(kd, cd, sd, GQA_DIM // 2).astype(BF16)
    vdt_ref[0] = _values_t_with_ones(y[:, 9 * gw + kvw:9 * gw + 2 * kvw], GQA_KV_HEADS).astype(BF16)


def _in_proj(xs, mod, w_in, rope, qg, kg, gate_b, gm):
    B, S, _ = xs.shape
    gw = GROUP_W
    kvw = GQA_KV_HEADS * GQA_DIM

    def tok(width):
        return pl.BlockSpec((1, TM, width), lambda b, j: (b, j, 0))

    def tok_t(rows):
        return pl.BlockSpec((1, rows, TM), lambda b, j: (b, 0, j))

    def table():
        return pl.BlockSpec((TM, LANES), lambda b, j: (j, 0))

    out_shapes = [
        ((B, gw, S), BF16, tok_t(gw)),
        ((B, S, gw), BF16, tok(gw)),
        ((B, DA_HEADS * VT_ROWS, S), BF16, tok_t(DA_HEADS * VT_ROWS)),
        ((B, S, gw), F32, tok(gw)),
        ((B, gw, S), BF16, tok_t(gw)),
        ((B, S, gw), BF16, tok(gw)),
        ((B, ML_HEADS * VT_ROWS, S), BF16, tok_t(ML_HEADS * VT_ROWS)),
        ((B, S, gw), BF16, tok(gw)),
        ((B, S, LANES), F32, tok(LANES)),
        ((B, 2 * ML_GATE_COLS, S), F32, tok_t(2 * ML_GATE_COLS)),
        ((B, gw, S), BF16, tok_t(gw)),
        ((B, S, kvw), BF16, tok(kvw)),
        ((B, GQA_KV_HEADS * VT_ROWS, S), BF16, tok_t(GQA_KV_HEADS * VT_ROWS)),
    ]
    return pl.pallas_call(
        _in_kernel,
        grid=(B, S // TM),
        in_specs=[tok(D_MODEL), _mod_spec(), _const_spec((D_MODEL, IN_W)),
                  table(), table(), table(), table(),
                  _const_spec((1, gw)), _const_spec((1, kvw)), _const_spec((1, LANES)),
                  _const_spec((gw, gw))],
        out_specs=[o[2] for o in out_shapes],
        out_shape=[jax.ShapeDtypeStruct(o[0], o[1]) for o in out_shapes],
        compiler_params=_cparams(("parallel", "parallel")),
        name="mix_in_proj",
    )(xs, mod, w_in, rope[0], rope[1], rope[2], rope[3], qg, kg, gate_b, gm)


def _flash_absorb(st, m_blk, vt, acc_ref, m_ref, c, first):
    if first:
        m_new = m_blk
    else:
        m_old = m_ref[c]
        m_new = jnp.maximum(m_old, m_blk)
    p = jnp.exp2((st - m_new).astype(BF16))
    pv = jnp.dot(vt, p, preferred_element_type=F32)
    if first:
        acc_ref[c] = pv
    else:
        acc_ref[c] = jnp.exp2(m_old - m_new) * acc_ref[c] + pv
    m_ref[c] = m_new


def _flash_sweep(chains, qw_ref, acc_ref, m_ref, st_ref, mb_ref):
    n_chains = len(chains)
    n_blk = (chains[0][0].shape[1] - TM) // KV_BLOCK

    def scores(rows, c):
        k_ref, l0, _, _ = chains[c]
        return jnp.dot(k_ref[0, rows, l0:l0 + LANES], qw_ref[c], preferred_element_type=F32)

    def vt_blk(c, start, size):
        _, _, vt_ref, r0 = chains[c]
        return vt_ref[0, r0:r0 + VT_ROWS, pl.ds(start, size)]

    def latent_start(i):
        start = TM + i * KV_BLOCK
        return start if isinstance(i, int) else pl.multiple_of(start, TM)

    def ctx_scores():
        return [scores(slice(0, TM), c) for c in range(n_chains)]

    def absorb_ctx(sts):
        for c in range(n_chains):
            _flash_absorb(sts[c], jnp.max(sts[c], axis=0, keepdims=True), vt_blk(c, 0, TM), acc_ref,
                          m_ref, c, True)

    def produce_scores(rows, slot, c):
        st = scores(rows, c)
        st_ref[slot, c] = st
        mb_ref[slot, c] = jnp.max(st, axis=0, keepdims=True)

    def run_half_step(i, parity):
        nxt = min(i, n_blk - 1) if isinstance(i, int) else jnp.minimum(i, n_blk - 1)
        rows = pl.ds(latent_start(nxt), KV_BLOCK)
        for c in range(n_chains):
            produce_scores(rows, parity, c)
            _flash_absorb(st_ref[1 - parity, c], mb_ref[1 - parity, c],
                          vt_blk(c, latent_start(i - 1), KV_BLOCK), acc_ref, m_ref, c, False)

    @pl.when(pl.program_id(1) == 0)
    def _():
        absorb_ctx(ctx_scores())

    @pl.when(pl.program_id(1) > 0)
    def _():
        sts = ctx_scores()
        for c in range(n_chains):
            produce_scores(pl.ds(latent_start(0), KV_BLOCK), 0, c)
        absorb_ctx(sts)

        def body(t, _):
            for u in range(ATTN_UNROLL):
                run_half_step(t * ATTN_UNROLL + u + 1, (u + 1) % 2)
            return 0

        n_full = n_blk // ATTN_UNROLL
        lax.fori_loop(0, n_full, body, 0)
        for u in range(n_blk % ATTN_UNROLL):
            run_half_step(n_full * ATTN_UNROLL + u + 1, (u + 1) % 2)


def _attn_out_t(acc):
    return acc[:HEAD_W] * (1.0 / acc[HEAD_W:HEAD_W + 1])


def _attn_kernel(qat_ref, qdt_ref, ka_ref, kd_ref, vat_ref, vdt_ref, lam_ref, g_ref, oa_ref, od_ref,
                 qw_ref, acc_ref, m_ref, st_ref, mb_ref, *, lam_init):
    n_diff = 2 * DA_HEADS
    per_slab = LANES // DA_QK
    group = GQA_HEADS // GQA_KV_HEADS
    feat = lax.broadcasted_iota(jnp.int32, (LANES, TM), 0)
    chains = []
    for c in range(n_diff):
        slab = c // per_slab
        qt = qat_ref[0, slab * LANES:(slab + 1) * LANES, :]
        qw_ref[c] = jnp.where(feat // DA_QK == c % per_slab, qt, jnp.zeros_like(qt))
        chains.append((ka_ref, LANES * slab, vat_ref, VT_ROWS * (c // 2)))
    for qh in range(GQA_HEADS):
        kvh = qh // group
        own = qdt_ref[0, qh * GQA_DIM:(qh + 1) * GQA_DIM, :]
        zeros = jnp.zeros_like(own)
        qw_ref[n_diff + qh] = jnp.concatenate([own if i == kvh else zeros
                                               for i in range(GQA_KV_HEADS)], axis=0)
        chains.append((kd_ref, 0, vdt_ref, VT_ROWS * kvh))
    _flash_sweep(chains, qw_ref, acc_ref, m_ref, st_ref, mb_ref)
    dl = lam_ref[...]
    lam = (jnp.exp(jnp.sum(dl[0:1] * dl[1:2], axis=1, keepdims=True))
           - jnp.exp(jnp.sum(dl[2:3] * dl[3:4], axis=1, keepdims=True)) + lam_init)
    outs = []
    for h in range(DA_HEADS):
        o = _attn_out_t(acc_ref[2 * h]) - lam * _attn_out_t(acc_ref[2 * h + 1])
        outs.append(o * lax.rsqrt(jnp.mean(o * o, axis=0, keepdims=True) + EPS))
    o_nat = jnp.concatenate(outs, axis=0).T
    oa_ref[0] = (o_nat * g_ref[...] * (1.0 - lam_init)).astype(BF16)
    o_t = jnp.concatenate([_attn_out_t(acc_ref[n_diff + qh]) for qh in range(GQA_HEADS)], axis=0)
    od_ref[0] = o_t.T.astype(BF16)


def _attention(qat, qdt, ka, kd, vat, vdt, diff_lambda, diff_norm_g, lam_init):
    B, S, gw = ka.shape
    n_chains = 2 * DA_HEADS + GQA_HEADS

    def q_tile():
        return pl.BlockSpec((1, gw, TM), lambda b, j: (b, 0, j))

    def resident(arr):
        return pl.BlockSpec((1,) + arr.shape[1:], lambda b, j: (b, 0, 0))

    def o_tile():
        return pl.BlockSpec((1, TM, gw), lambda b, j: (b, j, 0))

    return pl.pallas_call(
        functools.partial(_attn_kernel, lam_init=lam_init),
        grid=(B, S // TM),
        in_specs=[q_tile(), q_tile(), resident(ka), resident(kd), resident(vat), resident(vdt),
                  _const_spec(diff_lambda.shape), _const_spec(diff_norm_g.shape)],
        out_specs=[o_tile(), o_tile()],
        out_shape=[jax.ShapeDtypeStruct((B, S, gw), BF16)] * 2,
        scratch_shapes=[pltpu.VMEM((n_chains, LANES, TM), BF16),
                        pltpu.VMEM((n_chains, VT_ROWS, TM), F32),
                        pltpu.VMEM((n_chains, 1, TM), F32),
                        pltpu.VMEM((2, n_chains, KV_BLOCK, TM), F32),
                        pltpu.VMEM((2, n_chains, 1, TM), F32)],
        compiler_params=_cparams(("parallel", "arbitrary")),
        name="attention",
    )(qat, qdt, ka, kd, vat, vdt, diff_lambda, diff_norm_g)


def _mlstm_direction(k, qt, vt, g, gt, c_ref, m_ref, d, ht_ref):
    T = k.shape[0]
    key_pos = lax.broadcasted_iota(jnp.int32, (T, T), 0)
    qry_pos = lax.broadcasted_iota(jnp.int32, (T, T), 1)
    seen = (key_pos <= qry_pos) if d == 0 else (key_pos >= qry_pos)
    feat = lax.broadcasted_iota(jnp.int32, qt.shape, 0)
    last = T - 1 if d == 0 else 0
    outs = []
    for h in range(ML_HEADS):
        li_c = 2 * ML_HEADS * d + h
        b_c = ML_GATE_COLS + li_c + ML_HEADS
        sidx = d * ML_HEADS + h
        qtm = jnp.where(feat // ML_DIM == h, qt, jnp.zeros_like(qt))
        r_col = g[:, li_c:li_c + 1] - g[:, b_c:b_c + 1]
        b_row = gt[b_c:b_c + 1, :]
        li_row = gt[li_c:li_c + 1, :]
        m_prev = m_ref[sidx:sidx + 1, 0:1]
        c_prev = c_ref[d, h]
        vth = vt[VT_ROWS * h:VT_ROWS * (h + 1)]
        st = jnp.dot(k, qtm, preferred_element_type=F32)
        dmat = jnp.where(seen, b_row + r_col, -jnp.inf)
        inter = b_row + m_prev
        m_t = jnp.maximum(inter, jnp.max(dmat, axis=0, keepdims=True))
        a = (st * jnp.exp(dmat - m_t)).astype(BF16)
        w_inter = jnp.exp(inter - m_t)
        num = (jnp.dot(vth, a, preferred_element_type=F32)
               + w_inter * jnp.dot(c_prev.astype(BF16), qtm, preferred_element_type=F32))
        den = jnp.maximum(jnp.abs(num[ML_DIM:ML_DIM + 1]), jnp.exp(-m_t))
        outs.append(num[:ML_DIM] * (1.0 / den))
        b_end = b_row[:, last:last + 1]
        g_row = b_end - b_row + li_row
        m_new = jnp.maximum(b_end + m_prev, jnp.max(g_row, axis=1, keepdims=True))
        wk = jnp.exp(g_row - m_new)
        decay = jnp.exp(b_end + m_prev - m_new)
        vw = (vth.astype(F32) * wk).astype(BF16)
        c_ref[d, h] = decay * c_prev + jnp.dot(vw, k, preferred_element_type=F32)
        m_ref[sidx:sidx + 1, :] = jnp.broadcast_to(m_new, (1, LANES))
    ht_ref[0] = jnp.concatenate(outs, axis=0)


def _mlstm_kernel(kf_ref, qf_ref, vf_ref, gf_ref, gtf_ref, kb_ref, qb_ref, vb_ref, gb_ref, gtb_ref,
                  hf_ref, hb_ref, c_ref, m_ref):
    @pl.when(pl.program_id(1) == 0)
    def _():
        c_ref[...] = jnp.zeros_like(c_ref)
        m_ref[...] = jnp.zeros_like(m_ref)

    _mlstm_direction(kf_ref[0], qf_ref[0], vf_ref[0], gf_ref[0], gtf_ref[0], c_ref, m_ref, 0, hf_ref)
    _mlstm_direction(kb_ref[0], qb_ref[0], vb_ref[0], gb_ref[0], gtb_ref[0], c_ref, m_ref, 1, hb_ref)


def _mlstm(ck, cqt, cvt, gates, gates_t):
    B, S, gw = ck.shape
    nc = S // TM

    def fwd_chunk(i):
        return i

    def bwd_chunk(i):
        return jnp.where(i == 0, 0, nc - i)

    def specs(chunk):
        def tok(width):
            return pl.BlockSpec((1, TM, width), lambda b, i: (b, chunk(i), 0))

        def tok_t(rows):
            return pl.BlockSpec((1, rows, TM), lambda b, i: (b, 0, chunk(i)))

        return [tok(gw), tok_t(gw), tok_t(cvt.shape[1]), tok(LANES), tok_t(gates_t.shape[1])], tok_t(gw)

    in_f, out_f = specs(fwd_chunk)
    in_b, out_b = specs(bwd_chunk)
    return pl.pallas_call(
        _mlstm_kernel,
        grid=(B, nc),
        in_specs=in_f + in_b,
        out_specs=[out_f, out_b],
        out_shape=[jax.ShapeDtypeStruct((B, gw, S), F32)] * 2,
        scratch_shapes=[pltpu.VMEM((2, ML_HEADS, VT_ROWS, gw), F32),
                        pltpu.VMEM((2 * ML_HEADS, LANES), F32)],
        compiler_params=_cparams(("parallel", "arbitrary")),
        name="mlstm_scan",
    )(ck, cqt, cvt, gates, gates_t, ck, cqt, cvt, gates, gates_t)


def _pool_mixer(u_ref, up_ref, un_ref, ext_ref, pw_ref, ps_ref, first_tile):
    j = pl.program_id(1) + first_tile
    nt = pl.num_programs(1) + first_tile
    has_prev = j >= 2
    has_next = jnp.logical_and(j >= 1, j < nt - 1)
    u = u_ref[0]
    ext_ref[0:POOL_HALO, :] = jnp.where(has_prev, up_ref[0], 0.0)
    ext_ref[POOL_HALO:POOL_HALO + TM, :] = u
    ext_ref[POOL_HALO + TM:, :] = jnp.where(has_next, un_ref[0], 0.0)

    def shifted(s):
        return ext_ref[POOL_HALO + s:POOL_HALO + s + TM, :]

    lane = lax.broadcasted_iota(jnp.int32, (TM, GROUP_W), 1)
    r = lax.broadcasted_iota(jnp.int32, (TM, GROUP_W), 0)
    far = 2 * POOL_HALO
    left_room = r + jnp.where(has_prev, far, 0)
    right_room = (TM - 1 - r) + jnp.where(has_next, far, 0)
    total = u
    mean = jnp.zeros_like(u)
    prev_half = 0
    for gi, w in enumerate(POOL_WINDOWS):
        half = w // 2
        for s in list(range(-half, -prev_half)) + list(range(max(prev_half, 1), half)):
            total = total + shifted(s)
        prev_half = half
        cnt = jnp.minimum(left_room, half) + jnp.minimum(right_room, half - 1) + 1
        mean = jnp.where(lane // POOL_GROUP == gi, total / cnt.astype(F32), mean)
    dlt = (mean - u).astype(BF16)
    return jnp.dot(dlt, pw_ref[...], preferred_element_type=F32) * ps_ref[...]


def _out_kernel(x_ref, mod_ref, a_ref, u_ref, up_ref, un_ref, hf_ref, hb_ref, co_ref, d_ref,
                w_ref, pw_ref, ps_ref, mg_ref, gm_ref, g_ref, b_ref, o_ref, ext_ref, *, first_tile):
    gw = GROUP_W
    x = x_ref[0]
    mod = mod_ref[0]
    b_mix = _pool_mixer(u_ref, up_ref, un_ref, ext_ref, pw_ref, ps_ref, first_tile)
    hm = (hf_ref[0] + hb_ref[0]).T
    c_mix = (hm * lax.rsqrt(_group_mean(hm * hm, gm_ref[...]) + EPS) * mg_ref[...]
             * jax.nn.sigmoid(co_ref[0].astype(F32)))
    o = (jnp.dot(a_ref[0], w_ref[0:gw, :], preferred_element_type=F32)
         + jnp.dot(b_mix.astype(BF16), w_ref[gw:2 * gw, :], preferred_element_type=F32)
         + jnp.dot(c_mix.astype(BF16), w_ref[2 * gw:3 * gw, :], preferred_element_type=F32)
         + jnp.dot(d_ref[0], w_ref[3 * gw:4 * gw, :], preferred_element_type=F32))
    y = ALPHA * x + mod[5:6] * o
    o_ref[0] = _layer_norm(y) * g_ref[...] + b_ref[...]


def _out_proj(xs, mod, a, u, hf, hb, co, dd, w_out, pool_bd, pool_scale, ml_g, gm, g, b, keep_ctx):
    B, S, _ = xs.shape
    gw = GROUP_W
    hb_per_tile = TM // POOL_HALO
    n_halo = S // POOL_HALO
    t0 = 0 if keep_ctx else 1

    def tok(width):
        return pl.BlockSpec((1, TM, width), lambda b_, j: (b_, j + t0, 0))

    def tok_t(rows):
        return pl.BlockSpec((1, rows, TM), lambda b_, j: (b_, 0, j + t0))

    prev_spec = pl.BlockSpec((1, POOL_HALO, gw),
                             lambda b_, j: (b_, jnp.maximum((j + t0) * hb_per_tile - 1, 0), 0))
    next_spec = pl.BlockSpec((1, POOL_HALO, gw),
                             lambda b_, j: (b_, jnp.minimum((j + t0 + 1) * hb_per_tile, n_halo - 1), 0))
    return pl.pallas_call(
        functools.partial(_out_kernel, first_tile=t0),
        grid=(B, S // TM - t0),
        in_specs=[tok(D_MODEL), _mod_spec(keep_ctx), tok(gw), tok(gw), prev_spec, next_spec,
                  tok_t(gw), tok_t(gw), tok(gw), tok(gw),
                  _const_spec((D_MODEL, D_MODEL)), _const_spec((gw, gw)), _const_spec((1, gw)),
                  _const_spec((1, gw)), _const_spec((gw, gw)),
                  _const_spec((1, D_MODEL)), _const_spec((1, D_MODEL))],
        out_specs=pl.BlockSpec((1, TM, D_MODEL), lambda b_, j: (b_, j, 0)),
        out_shape=jax.ShapeDtypeStruct((B, S - t0 * TM, D_MODEL), F32),
        scratch_shapes=[pltpu.VMEM((TM + 2 * POOL_HALO, gw), F32)],
        compiler_params=_cparams(("parallel", "parallel")),
        name="mix_out_proj",
    )(xs, mod, a, u, u, u, hf, hb, co, dd, w_out, pool_bd, pool_scale, ml_g, gm, g, b)


def _rope_tables(L, Lc, dim):
    rows = L // GRID_W
    row = jnp.repeat(jnp.arange(rows), GRID_W).astype(F32)
    col = jnp.tile(jnp.arange(GRID_W), rows).astype(F32)
    axis_dim = dim // 2
    inv = ROPE_THETA ** (-jnp.arange(0, axis_dim, 2, dtype=F32) / axis_dim)
    ang = jnp.concatenate([row[:, None] * inv, col[:, None] * inv], axis=-1)
    cos, sin = jnp.cos(ang), jnp.sin(ang)
    reps = LANES // dim
    cos_p = jnp.tile(jnp.concatenate([cos, cos], axis=-1), (1, reps))
    sin_p = jnp.tile(jnp.concatenate([-sin, sin], axis=-1), (1, reps))
    cos_p = jnp.concatenate([jnp.ones((Lc, LANES), F32), cos_p], axis=0)
    sin_p = jnp.concatenate([jnp.zeros((Lc, LANES), F32), sin_p], axis=0)
    return cos_p, sin_p


def _reorder_w_in(w):
    g0 = 8 * GROUP_W
    g1 = g0 + 4 * ML_HEADS
    pad = jnp.zeros((w.shape[0], LANES - 4 * ML_HEADS), w.dtype)
    return jnp.concatenate([w[:, :g0], w[:, g1:], w[:, g0:g1], pad], axis=1).astype(BF16)


def _block_diag(blocks):
    n = blocks.shape[0]
    rows = []
    for i in range(n):
        rows.append(jnp.concatenate([blocks[i] if k == i else jnp.zeros_like(blocks[i])
                                     for k in range(n)], axis=1))
    return jnp.concatenate(rows, axis=0)


def kernel(x, c, ctx, c_ctx, w_ada, b_ada, ln_g, ln_b, ffn1_wi, ffn1_wo, ffn2_wi, ffn2_wo, w_in, w_out,
           diff_lambda, diff_norm_g, pool_w, pool_scale, ml_gate_b, ml_norm_g, gqa_qnorm_g, gqa_knorm_g):
    B, L, _ = x.shape
    Lc = ctx.shape[1]
    assert Lc == TM and L % KV_BLOCK == 0 and L % GRID_W == 0 and B + 1 <= MOD_ROWS
    depth = w_ada.shape[0]

    cc = jnp.concatenate([c_ctx[None], c, jnp.zeros((MOD_ROWS - 1 - B, D_MODEL), F32)], axis=0)
    mod_all = _modulation(cc, w_ada, b_ada).reshape(depth, MOD_ROWS, N_MOD, D_MODEL)

    rope = _rope_tables(L, Lc, DA_QK) + _rope_tables(L, Lc, GQA_DIM)
    gm = _block_diag(jnp.full((GROUP_W // HEAD_W, HEAD_W, HEAD_W), 1.0 / HEAD_W, BF16))

    xs = (ctx, x)
    for l in range(depth):
        last = l == depth - 1
        mod = mod_all[l]
        lam_init = 0.8 - 0.6 * math.exp(-0.3 * l)
        xs = _ffn(xs, mod, ffn1_wi[l].astype(BF16), ffn1_wo[l].astype(BF16), ln_g[l, 0], ln_b[l, 0], 0)
        gate_b = jnp.concatenate([ml_gate_b[l].reshape(1, -1),
                                  jnp.zeros((1, LANES - 4 * ML_HEADS), F32)], axis=1)
        qg = jnp.tile(gqa_qnorm_g[l], GQA_HEADS).reshape(1, -1)
        kg = jnp.tile(gqa_knorm_g[l], GQA_KV_HEADS).reshape(1, -1)
        (qat, ka, vat, u, cqt, ck, cvt, co, gates, gates_t, qdt, kd, vdt) = _in_proj(
            xs, mod, _reorder_w_in(w_in[l]), rope, qg, kg, gate_b, gm)
        a, dd = _attention(qat, qdt, ka, kd, vat, vdt, diff_lambda[l], diff_norm_g[l].reshape(1, -1),
                           lam_init)
        hf, hb = _mlstm(ck, cqt, cvt, gates, gates_t)
        xs = _out_proj(xs, mod, a, u, hf, hb, co, dd, w_out[l].astype(BF16),
                       _block_diag(pool_w[l]).astype(BF16), pool_scale[l].reshape(1, -1),
                       ml_norm_g[l].reshape(1, -1), gm, ln_g[l, 1].reshape(1, -1),
                       ln_b[l, 1].reshape(1, -1), keep_ctx=not last)
        xs = _ffn(xs, mod, ffn2_wi[l].astype(BF16), ffn2_wo[l].astype(BF16), ln_g[l, 2], ln_b[l, 2], 2,
                  has_ctx=not last)
    return xs
```

```python
import functools
import math

import jax
import jax.numpy as jnp
from jax import lax
from jax.experimental import pallas as pl
from jax.experimental.pallas import tpu as pltpu

F32 = jnp.float32
BF16 = jnp.bfloat16

D_MODEL = 1024
DEPTH = 2
GRID_W = 64
GROUP_W = 256
D_FF = 2816
N_MOD = 9
EPS = 1e-6
ROPE_THETA = 10000.0
DA_HEADS = 4
DA_QK = 32
DA_V = 64
POOL_WINDOWS = (2, 4, 8, 16)
POOL_GROUP = 64
POOL_HALO = 8
ML_HEADS = 4
ML_DIM = 64
ML_GATE_COLS = 4 * ML_HEADS
GQA_HEADS = 4
GQA_KV_HEADS = 2
GQA_DIM = 64
HEAD_W = 64
LANES = 128
BF16_SUBLANES = 16
VT_ROWS = HEAD_W + BF16_SUBLANES
ALPHA = (2.0 * DEPTH) ** 0.25
LOG2E = math.log2(math.e)

TM = 256
KV_BLOCK = 256
ATTN_UNROLL = 16
FF_CHUNKS = ((0, 1024), (1024, 1024), (2048, 768))
IN_W = 2688
MOD_ROWS = 16
MOD_TN = 1152
VMEM_LIMIT = 56 * 1024 * 1024


def _cparams(sem):
    return pltpu.CompilerParams(dimension_semantics=sem, vmem_limit_bytes=VMEM_LIMIT)


def _const_spec(shape):
    nd = len(shape)
    return pl.BlockSpec(shape, lambda b, j: (0,) * nd, pipeline_mode=pl.Buffered(1))


def _mod_spec(has_ctx=True):
    if has_ctx:
        return pl.BlockSpec((1, N_MOD, D_MODEL), lambda b, j: (jnp.where(j == 0, 0, b + 1), 0, 0))
    return pl.BlockSpec((1, N_MOD, D_MODEL), lambda b, j: (b + 1, 0, 0))


def _layer_norm(x):
    mu = jnp.mean(x, axis=-1, keepdims=True)
    xc = x - mu
    var = jnp.mean(xc * xc, axis=-1, keepdims=True)
    return xc * lax.rsqrt(var + EPS)


def _group_mean(x, gm):
    hi = x.astype(BF16)
    lo = (x - hi.astype(F32)).astype(BF16)
    return (jnp.dot(hi, gm, preferred_element_type=F32) + jnp.dot(lo, gm, preferred_element_type=F32))


def _mod_kernel(c_ref, w_ref, b_ref, o_ref):
    c = c_ref[...]
    s = c * jax.nn.sigmoid(c)
    o_ref[0] = jnp.dot(s, w_ref[0], preferred_element_type=F32,
                       precision=lax.Precision.HIGHEST) + b_ref[0]


def _modulation(cc, w_ada, b_ada):
    depth = w_ada.shape[0]
    n = N_MOD * D_MODEL
    return pl.pallas_call(
        _mod_kernel,
        grid=(depth, n // MOD_TN),
        in_specs=[pl.BlockSpec((MOD_ROWS, D_MODEL), lambda l, j: (0, 0)),
                  pl.BlockSpec((1, D_MODEL, MOD_TN), lambda l, j: (l, 0, j)),
                  pl.BlockSpec((1, 1, MOD_TN), lambda l, j: (l, 0, j))],
        out_specs=pl.BlockSpec((1, MOD_ROWS, MOD_TN), lambda l, j: (l, 0, j)),
        out_shape=jax.ShapeDtypeStruct((depth, MOD_ROWS, n), F32),
        compiler_params=_cparams(("parallel", "parallel")),
        name="adaln_mod",
    )(cc, w_ada, b_ada.reshape(depth, 1, n))


def _ffn_kernel(*refs, s, split_input, tiles_per_sample):
    n_data = 4 if split_input else 1
    data = refs[:n_data]
    mod0_ref, mod1_ref, wi_ref, wo_ref, g_ref, b_ref, o_ref = refs[n_data:]
    xs = []
    for half in range(2):
        if split_input:
            j = (2 * pl.program_id(0) + half) % tiles_per_sample
            xs.append(jnp.where(j == 0, data[2 * half][0], data[2 * half + 1][0]))
        else:
            xs.append(data[0][half])
    mods = [mod0_ref[0], mod1_ref[0]]
    hs = [(_layer_norm(x) * (1.0 + mod[3 * s + 1:3 * s + 2]) + mod[3 * s:3 * s + 1]).astype(BF16)
          for x, mod in zip(xs, mods)]
    accs = []
    for h in hs:
        acc = jnp.zeros((TM, D_MODEL), F32)
        for c0, cw in FF_CHUNKS:
            gate = jnp.dot(h, wi_ref[:, c0:c0 + cw], preferred_element_type=F32)
            up = jnp.dot(h, wi_ref[:, D_FF + c0:D_FF + c0 + cw], preferred_element_type=F32)
            act = (gate * jax.nn.sigmoid(gate) * up).astype(BF16)
            acc = acc + jnp.dot(act, wo_ref[c0:c0 + cw, :], preferred_element_type=F32)
        accs.append(acc)
    for half, (x, mod, acc) in enumerate(zip(xs, mods, accs)):
        y = ALPHA * x + (0.5 * mod[3 * s + 2:3 * s + 3]) * acc
        o_ref[half] = _layer_norm(y) * g_ref[...] + b_ref[...]


def _ffn(xs, mod, wi, wo, g, b, s, has_ctx=True):
    split_input = isinstance(xs, tuple)
    if split_input:
        ctx, x = xs
        B, S = x.shape[0], ctx.shape[1] + x.shape[1]
    else:
        B, S, _ = xs.shape
    n = S // TM
    assert (B * n) % 2 == 0

    def sample_tile(p, half):
        sub = 2 * p + half
        return sub // n, sub % n

    def mod_spec(half):
        def index(p):
            b_, j = sample_tile(p, half)
            return (jnp.where(jnp.logical_and(has_ctx, j == 0), 0, b_ + 1), 0, 0)
        return pl.BlockSpec((1, N_MOD, D_MODEL), index)

    def const(shape):
        return pl.BlockSpec(shape, lambda p: (0,) * len(shape), pipeline_mode=pl.Buffered(1))

    if split_input:
        data, data_specs = [], []
        for half in range(2):
            data += [ctx, x]
            data_specs += [
                pl.BlockSpec((1, TM, D_MODEL), lambda p, h=half: (sample_tile(p, h)[0], 0, 0)),
                pl.BlockSpec((1, TM, D_MODEL),
                             lambda p, h=half: (sample_tile(p, h)[0],
                                                jnp.maximum(sample_tile(p, h)[1] - 1, 0), 0))]
    else:
        data = [xs.reshape(B * n, TM, D_MODEL)]
        data_specs = [pl.BlockSpec((2, TM, D_MODEL), lambda p: (p, 0, 0))]
    out = pl.pallas_call(
        functools.partial(_ffn_kernel, s=s, split_input=split_input, tiles_per_sample=n),
        grid=(B * n // 2,),
        in_specs=data_specs + [mod_spec(0), mod_spec(1), const((D_MODEL, 2 * D_FF)),
                               const((D_FF, D_MODEL)), const((1, D_MODEL)), const((1, D_MODEL))],
        out_specs=pl.BlockSpec((2, TM, D_MODEL), lambda p: (p, 0, 0)),
        out_shape=jax.ShapeDtypeStruct((B * n, TM, D_MODEL), F32),
        compiler_params=_cparams(("parallel",)),
        name="macaron_ffn",
    )(*data, mod, mod, wi, wo, g.reshape(1, D_MODEL), b.reshape(1, D_MODEL))
    return out.reshape(B, S, D_MODEL)


def _rope(x, cos, sin_signed, half):
    w = x.shape[1]
    reps = w // LANES
    if reps > 1:
        cos = jnp.concatenate([cos] * reps, axis=1)
        sin_signed = jnp.concatenate([sin_signed] * reps, axis=1)
    lane = lax.broadcasted_iota(jnp.int32, x.shape, 1)
    first = (lane % (2 * half)) < half
    partner = jnp.where(first, pltpu.roll(x, w - half, 1), pltpu.roll(x, half, 1))
    return x * cos + partner * sin_signed


def _log_sigmoid(x):
    return jnp.minimum(x, 0.0) - jnp.log1p(jnp.exp(-jnp.abs(x)))


def _values_t_with_ones(v, n_heads):
    vt = v.T
    ones = jnp.ones((VT_ROWS - HEAD_W, v.shape[0]), v.dtype)
    parts = []
    for h in range(n_heads):
        parts += [vt[h * HEAD_W:(h + 1) * HEAD_W], ones]
    return jnp.concatenate(parts, axis=0)


def _mlstm_gate_slab(graw):
    t = graw.shape[0]
    lane = lax.broadcasted_iota(jnp.int32, graw.shape, 1)
    is_forget = ((lane // ML_HEADS) % 2) == 1
    gl = jnp.where(is_forget, _log_sigmoid(graw), graw)
    row = lax.broadcasted_iota(jnp.int32, (t, t), 0)
    col = lax.broadcasted_iota(jnp.int32, (t, t), 1)
    tri = jnp.where(col <= row, 1.0, 0.0).astype(BF16)
    hi = gl.astype(BF16)
    r1 = gl - hi.astype(F32)
    mid = r1.astype(BF16)
    lo = (r1 - mid.astype(F32)).astype(BF16)
    csum = (jnp.dot(tri, hi, preferred_element_type=F32) + jnp.dot(tri, mid, preferred_element_type=F32)
            + jnp.dot(tri, lo, preferred_element_type=F32))
    rsum = csum[t - 1:t] - csum + gl
    cum = jnp.where(lane >= 2 * ML_HEADS, rsum, csum)
    return jnp.where(lane < ML_GATE_COLS, gl, pltpu.roll(cum, ML_GATE_COLS, 1))


def _in_kernel(x_ref, mod_ref, w_ref, cos_a, sin_a, cos_d, sin_d, qg_ref, kg_ref, gb_ref, gm_ref,
               qat_ref, ka_ref, vat_ref, u_ref, cqt_ref, ck_ref, cvt_ref, co_ref, g_ref, gt_ref,
               qdt_ref, kd_ref, vdt_ref):
    x = x_ref[0]
    mod = mod_ref[0]
    h = (_layer_norm(x) * (1.0 + mod[4:5]) + mod[3:4]).astype(BF16)
    y = jnp.dot(h, w_ref[...], preferred_element_type=F32)
    gw = GROUP_W
    ca, sa = cos_a[...], sin_a[...]
    qa = _rope(y[:, 0:gw], ca, sa, DA_QK // 2) * (DA_QK ** -0.5 * LOG2E)
    qat_ref[0] = qa.T.astype(BF16)
    ka_ref[0] = _rope(y[:, gw:2 * gw], ca, sa, DA_QK // 2).astype(BF16)
    vat_ref[0] = _values_t_with_ones(y[:, 2 * gw:3 * gw], DA_HEADS).astype(BF16)
    u_ref[0] = y[:, 3 * gw:4 * gw]
    cqt_ref[0] = (y[:, 4 * gw:5 * gw] * ML_DIM ** -0.5).T.astype(BF16)
    ck_ref[0] = y[:, 5 * gw:6 * gw].astype(BF16)
    cvt_ref[0] = _values_t_with_ones(y[:, 6 * gw:7 * gw], ML_HEADS).astype(BF16)
    co_ref[0] = y[:, 7 * gw:8 * gw].astype(BF16)
    gates = _mlstm_gate_slab(y[:, 10 * gw:10 * gw + LANES] + gb_ref[...])
    g_ref[0] = gates
    gt_ref[0] = gates.T[0:2 * ML_GATE_COLS]
    gm = gm_ref[...]
    cd, sd = cos_d[...], sin_d[...]
    qd = y[:, 8 * gw:9 * gw]
    qd = qd * lax.rsqrt(_group_mean(qd * qd, gm) + EPS) * qg_ref[...]
    qdt_ref[0] = (_rope(qd, cd, sd, GQA_DIM // 2) * (GQA_DIM ** -0.5 * LOG2E)).T.astype(BF16)
    kvw = GQA_KV_HEADS * GQA_DIM
    kd = y[:, 9 * gw:9 * gw + kvw]
    kd = kd * lax.rsqrt(_group_mean(kd * kd, gm[:kvw, :kvw]) + EPS) * kg_ref[...]
    kd_ref[0] = _rope(kd, cd, sd, GQA_DIM // 2).astype(BF16)
    vdt_ref[0] = _values_t_with_ones(y[:, 9 * gw + kvw:9 * gw + 2 * kvw], GQA_KV_HEADS).astype(BF16)


def _in_proj(xs, mod, w_in, rope, qg, kg, gate_b, gm):
    B, S, _ = xs.shape
    gw = GROUP_W
    kvw = GQA_KV_HEADS * GQA_DIM

    def tok(width):
        return pl.BlockSpec((1, TM, width), lambda b, j: (b, j, 0))

    def tok_t(rows):
        return pl.BlockSpec((1, rows, TM), lambda b, j: (b, 0, j))

    def table():
        return pl.BlockSpec((TM, LANES), lambda b, j: (j, 0))

    out_shapes = [
        ((B, gw, S), BF16, tok_t(gw)),
        ((B, S, gw), BF16, tok(gw)),
        ((B, DA_HEADS * VT_ROWS, S), BF16, tok_t(DA_HEADS * VT_ROWS)),
        ((B, S, gw), F32, tok(gw)),
        ((B, gw, S), BF16, tok_t(gw)),
        ((B, S, gw), BF16, tok(gw)),
        ((B, ML_HEADS * VT_ROWS, S), BF16, tok_t(ML_HEADS * VT_ROWS)),
        ((B, S, gw), BF16, tok(gw)),
        ((B, S, LANES), F32, tok(LANES)),
        ((B, 2 * ML_GATE_COLS, S), F32, tok_t(2 * ML_GATE_COLS)),
        ((B, gw, S), BF16, tok_t(gw)),
        ((B, S, kvw), BF16, tok(kvw)),
        ((B, GQA_KV_HEADS * VT_ROWS, S), BF16, tok_t(GQA_KV_HEADS * VT_ROWS)),
    ]
    return pl.pallas_call(
        _in_kernel,
        grid=(B, S // TM),
        in_specs=[tok(D_MODEL), _mod_spec(), _const_spec((D_MODEL, IN_W)),
                  table(), table(), table(), table(),
                  _const_spec((1, gw)), _const_spec((1, kvw)), _const_spec((1, LANES)),
                  _const_spec((gw, gw))],
        out_specs=[o[2] for o in out_shapes],
        out_shape=[jax.ShapeDtypeStruct(o[0], o[1]) for o in out_shapes],
        compiler_params=_cparams(("parallel", "parallel")),
        name="mix_in_proj",
    )(xs, mod, w_in, rope[0], rope[1], rope[2], rope[3], qg, kg, gate_b, gm)


def _flash_absorb(st, m_blk, vt, acc_ref, m_ref, c, first):
    if first:
        m_new = m_blk
    else:
        m_old = m_ref[c]
        m_new = jnp.maximum(m_old, m_blk)
    p = jnp.exp2((st - m_new).astype(BF16))
    pv = jnp.dot(vt, p, preferred_element_type=F32)
    if first:
        acc_ref[c] = pv
    else:
        acc_ref[c] = jnp.exp2(m_old - m_new) * acc_ref[c] + pv
    m_ref[c] = m_new


def _flash_sweep(chains, qw_ref, acc_ref, m_ref, st_ref, mb_ref):
    n_chains = len(chains)
    n_blk = (chains[0][0].shape[1] - TM) // KV_BLOCK

    def scores(rows, c):
        k_ref, l0, _, _ = chains[c]
        return jnp.dot(k_ref[0, rows, l0:l0 + LANES], qw_ref[c], preferred_element_type=F32)

    def vt_blk(c, start, size):
        _, _, vt_ref, r0 = chains[c]
        return vt_ref[0, r0:r0 + VT_ROWS, pl.ds(start, size)]

    def latent_start(i):
        start = TM + i * KV_BLOCK
        return start if isinstance(i, int) else pl.multiple_of(start, TM)

    def ctx_scores():
        return [scores(slice(0, TM), c) for c in range(n_chains)]

    def absorb_ctx(sts):
        for c in range(n_chains):
            _flash_absorb(sts[c], jnp.max(sts[c], axis=0, keepdims=True), vt_blk(c, 0, TM), acc_ref,
                          m_ref, c, True)

    def produce_scores(rows, slot, c):
        st = scores(rows, c)
        st_ref[slot, c] = st
        mb_ref[slot, c] = jnp.max(st, axis=0, keepdims=True)

    def run_half_step(i, parity):
        nxt = min(i, n_blk - 1) if isinstance(i, int) else jnp.minimum(i, n_blk - 1)
        rows = pl.ds(latent_start(nxt), KV_BLOCK)
        for c in range(n_chains):
            produce_scores(rows, parity, c)
            _flash_absorb(st_ref[1 - parity, c], mb_ref[1 - parity, c],
                          vt_blk(c, latent_start(i - 1), KV_BLOCK), acc_ref, m_ref, c, False)

    @pl.when(pl.program_id(1) == 0)
    def _():
        absorb_ctx(ctx_scores())

    @pl.when(pl.program_id(1) > 0)
    def _():
        sts = ctx_scores()
        for c in range(n_chains):
            produce_scores(pl.ds(latent_start(0), KV_BLOCK), 0, c)
        absorb_ctx(sts)

        def body(t, _):
            for u in range(ATTN_UNROLL):
                run_half_step(t * ATTN_UNROLL + u + 1, (u + 1) % 2)
            return 0

        n_full = n_blk // ATTN_UNROLL
        lax.fori_loop(0, n_full, body, 0)
        for u in range(n_blk % ATTN_UNROLL):
            run_half_step(n_full * ATTN_UNROLL + u + 1, (u + 1) % 2)


def _attn_out_t(acc):
    return acc[:HEAD_W] * (1.0 / acc[HEAD_W:HEAD_W + 1])


def _attn_kernel(qat_ref, qdt_ref, ka_ref, kd_ref, vat_ref, vdt_ref, lam_ref, g_ref, oa_ref, od_ref,
                 qw_ref, acc_ref, m_ref, st_ref, mb_ref, *, lam_init):
    n_diff = 2 * DA_HEADS
    per_slab = LANES // DA_QK
    group = GQA_HEADS // GQA_KV_HEADS
    feat = lax.broadcasted_iota(jnp.int32, (LANES, TM), 0)
    chains = []
    for c in range(n_diff):
        slab = c // per_slab
        qt = qat_ref[0, slab * LANES:(slab + 1) * LANES, :]
        qw_ref[c] = jnp.where(feat // DA_QK == c % per_slab, qt, jnp.zeros_like(qt))
        chains.append((ka_ref, LANES * slab, vat_ref, VT_ROWS * (c // 2)))
    for qh in range(GQA_HEADS):
        kvh = qh // group
        own = qdt_ref[0, qh * GQA_DIM:(qh + 1) * GQA_DIM, :]
        zeros = jnp.zeros_like(own)
        qw_ref[n_diff + qh] = jnp.concatenate([own if i == kvh else zeros
                                               for i in range(GQA_KV_HEADS)], axis=0)
        chains.append((kd_ref, 0, vdt_ref, VT_ROWS * kvh))
    _flash_sweep(chains, qw_ref, acc_ref, m_ref, st_ref, mb_ref)
    dl = lam_ref[...]
    lam = (jnp.exp(jnp.sum(dl[0:1] * dl[1:2], axis=1, keepdims=True))
           - jnp.exp(jnp.sum(dl[2:3] * dl[3:4], axis=1, keepdims=True)) + lam_init)
    outs = []
    for h in range(DA_HEADS):
        o = _attn_out_t(acc_ref[2 * h]) - lam * _attn_out_t(acc_ref[2 * h + 1])
        outs.append(o * lax.rsqrt(jnp.mean(o * o, axis=0, keepdims=True) + EPS))
    o_nat = jnp.concatenate(outs, axis=0).T
    oa_ref[0] = (o_nat * g_ref[...] * (1.0 - lam_init)).astype(BF16)
    o_t = jnp.concatenate([_attn_out_t(acc_ref[n_diff + qh]) for qh in range(GQA_HEADS)], axis=0)
    od_ref[0] = o_t.T.astype(BF16)


def _attention(qat, qdt, ka, kd, vat, vdt, diff_lambda, diff_norm_g, lam_init):
    B, S, gw = ka.shape
    n_chains = 2 * DA_HEADS + GQA_HEADS

    def q_tile():
        return pl.BlockSpec((1, gw, TM), lambda b, j: (b, 0, j))

    def resident(arr):
        return pl.BlockSpec((1,) + arr.shape[1:], lambda b, j: (b, 0, 0))

    def o_tile():
        return pl.BlockSpec((1, TM, gw), lambda b, j: (b, j, 0))

    return pl.pallas_call(
        functools.partial(_attn_kernel, lam_init=lam_init),
        grid=(B, S // TM),
        in_specs=[q_tile(), q_tile(), resident(ka), resident(kd), resident(vat), resident(vdt),
                  _const_spec(diff_lambda.shape), _const_spec(diff_norm_g.shape)],
        out_specs=[o_tile(), o_tile()],
        out_shape=[jax.ShapeDtypeStruct((B, S, gw), BF16)] * 2,
        scratch_shapes=[pltpu.VMEM((n_chains, LANES, TM), BF16),
                        pltpu.VMEM((n_chains, VT_ROWS, TM), F32),
                        pltpu.VMEM((n_chains, 1, TM), F32),
                        pltpu.VMEM((2, n_chains, KV_BLOCK, TM), F32),
                        pltpu.VMEM((2, n_chains, 1, TM), F32)],
        compiler_params=_cparams(("parallel", "arbitrary")),
        name="attention",
    )(qat, qdt, ka, kd, vat, vdt, diff_lambda, diff_norm_g)


def _mlstm_direction(k, qt, vt, g, gt, c_ref, m_ref, d, ht_ref):
    T = k.shape[0]
    key_pos = lax.broadcasted_iota(jnp.int32, (T, T), 0)
    qry_pos = lax.broadcasted_iota(jnp.int32, (T, T), 1)
    seen = (key_pos <= qry_pos) if d == 0 else (key_pos >= qry_pos)
    feat = lax.broadcasted_iota(jnp.int32, qt.shape, 0)
    last = T - 1 if d == 0 else 0
    outs = []
    early = []
    for h in range(ML_HEADS):
        qtm = jnp.where(feat // ML_DIM == h, qt, jnp.zeros_like(qt))
        early.append((jnp.dot(k, qtm, preferred_element_type=F32),
                      jnp.dot(c_ref[d, h].astype(BF16), qtm, preferred_element_type=F32)))
    for h in range(ML_HEADS):
        li_c = 2 * ML_HEADS * d + h
        b_c = ML_GATE_COLS + li_c + ML_HEADS
        sidx = d * ML_HEADS + h
        st, inter_num = early[h]
        r_col = g[:, li_c:li_c + 1] - g[:, b_c:b_c + 1]
        b_row = gt[b_c:b_c + 1, :]
        li_row = gt[li_c:li_c + 1, :]
        m_prev = m_ref[sidx:sidx + 1, 0:1]
        c_prev = c_ref[d, h]
        vth = vt[VT_ROWS * h:VT_ROWS * (h + 1)]
        dmat = jnp.where(seen, b_row + r_col, -jnp.inf)
        inter = b_row + m_prev
        m_t = jnp.maximum(inter, jnp.max(dmat, axis=0, keepdims=True))
        a = (st * jnp.exp(dmat - m_t)).astype(BF16)
        w_inter = jnp.exp(inter - m_t)
        num = jnp.dot(vth, a, preferred_element_type=F32) + w_inter * inter_num
        den = jnp.maximum(jnp.abs(num[ML_DIM:ML_DIM + 1]), jnp.exp(-m_t))
        outs.append(num[:ML_DIM] * (1.0 / den))
        b_end = b_row[:, last:last + 1]
        g_row = b_end - b_row + li_row
        m_new = jnp.maximum(b_end + m_prev, jnp.max(g_row, axis=1, keepdims=True))
        wk = jnp.exp(g_row - m_new)
        decay = jnp.exp(b_end + m_prev - m_new)
        vw = (vth.astype(F32) * wk).astype(BF16)
        c_ref[d, h] = decay * c_prev + jnp.dot(vw, k, preferred_element_type=F32)
        m_ref[sidx:sidx + 1, :] = jnp.broadcast_to(m_new, (1, LANES))
    ht_ref[0] = jnp.concatenate(outs, axis=0)


def _mlstm_kernel(kf_ref, qf_ref, vf_ref, gf_ref, gtf_ref, kb_ref, qb_ref, vb_ref, gb_ref, gtb_ref,
                  hf_ref, hb_ref, c_ref, m_ref):
    @pl.when(pl.program_id(1) == 0)
    def _():
        c_ref[...] = jnp.zeros_like(c_ref)
        m_ref[...] = jnp.zeros_like(m_ref)

    _mlstm_direction(kf_ref[0], qf_ref[0], vf_ref[0], gf_ref[0], gtf_ref[0], c_ref, m_ref, 0, hf_ref)
    _mlstm_direction(kb_ref[0], qb_ref[0], vb_ref[0], gb_ref[0], gtb_ref[0], c_ref, m_ref, 1, hb_ref)


def _mlstm(ck, cqt, cvt, gates, gates_t):
    B, S, gw = ck.shape
    nc = S // TM

    def fwd_chunk(i):
        return i

    def bwd_chunk(i):
        return jnp.where(i == 0, 0, nc - i)

    def specs(chunk):
        def tok(width):
            return pl.BlockSpec((1, TM, width), lambda b, i: (b, chunk(i), 0))

        def tok_t(rows):
            return pl.BlockSpec((1, rows, TM), lambda b, i: (b, 0, chunk(i)))

        return [tok(gw), tok_t(gw), tok_t(cvt.shape[1]), tok(LANES), tok_t(gates_t.shape[1])], tok_t(gw)

    in_f, out_f = specs(fwd_chunk)
    in_b, out_b = specs(bwd_chunk)
    return pl.pallas_call(
        _mlstm_kernel,
        grid=(B, nc),
        in_specs=in_f + in_b,
        out_specs=[out_f, out_b],
        out_shape=[jax.ShapeDtypeStruct((B, gw, S), F32)] * 2,
        scratch_shapes=[pltpu.VMEM((2, ML_HEADS, VT_ROWS, gw), F32),
                        pltpu.VMEM((2 * ML_HEADS, LANES), F32)],
        compiler_params=_cparams(("parallel", "arbitrary")),
        name="mlstm_scan",
    )(ck, cqt, cvt, gates, gates_t, ck, cqt, cvt, gates, gates_t)


def _pool_mixer(u_ref, up_ref, un_ref, ext_ref, pw_ref, ps_ref, first_tile):
    j = pl.program_id(1) + first_tile
    nt = pl.num_programs(1) + first_tile
    has_prev = j >= 2
    has_next = jnp.logical_and(j >= 1, j < nt - 1)
    u = u_ref[0]
    ext_ref[0:POOL_HALO, :] = jnp.where(has_prev, up_ref[0], 0.0)
    ext_ref[POOL_HALO:POOL_HALO + TM, :] = u
    ext_ref[POOL_HALO + TM:, :] = jnp.where(has_next, un_ref[0], 0.0)

    def shifted(s):
        return ext_ref[POOL_HALO + s:POOL_HALO + s + TM, :]

    lane = lax.broadcasted_iota(jnp.int32, (TM, GROUP_W), 1)
    r = lax.broadcasted_iota(jnp.int32, (TM, GROUP_W), 0)
    far = 2 * POOL_HALO
    left_room = r + jnp.where(has_prev, far, 0)
    right_room = (TM - 1 - r) + jnp.where(has_next, far, 0)
    total = u
    mean = jnp.zeros_like(u)
    prev_half = 0
    for gi, w in enumerate(POOL_WINDOWS):
        half = w // 2
        for s in list(range(-half, -prev_half)) + list(range(max(prev_half, 1), half)):
            total = total + shifted(s)
        prev_half = half
        cnt = jnp.minimum(left_room, half) + jnp.minimum(right_room, half - 1) + 1
        mean = jnp.where(lane // POOL_GROUP == gi, total / cnt.astype(F32), mean)
    dlt = (mean - u).astype(BF16)
    return jnp.dot(dlt, pw_ref[...], preferred_element_type=F32) * ps_ref[...]


def _out_kernel(x_ref, mod_ref, a_ref, u_ref, up_ref, un_ref, hf_ref, hb_ref, co_ref, d_ref,
                w_ref, pw_ref, ps_ref, mg_ref, gm_ref, g_ref, b_ref, o_ref, ext_ref, *, first_tile):
    gw = GROUP_W
    x = x_ref[0]
    mod = mod_ref[0]
    b_mix = _pool_mixer(u_ref, up_ref, un_ref, ext_ref, pw_ref, ps_ref, first_tile)
    hm = (hf_ref[0] + hb_ref[0]).T
    c_mix = (hm * lax.rsqrt(_group_mean(hm * hm, gm_ref[...]) + EPS) * mg_ref[...]
             * jax.nn.sigmoid(co_ref[0].astype(F32)))
    o = (jnp.dot(a_ref[0], w_ref[0:gw, :], preferred_element_type=F32)
         + jnp.dot(b_mix.astype(BF16), w_ref[gw:2 * gw, :], preferred_element_type=F32)
         + jnp.dot(c_mix.astype(BF16), w_ref[2 * gw:3 * gw, :], preferred_element_type=F32)
         + jnp.dot(d_ref[0], w_ref[3 * gw:4 * gw, :], preferred_element_type=F32))
    y = ALPHA * x + mod[5:6] * o
    o_ref[0] = _layer_norm(y) * g_ref[...] + b_ref[...]


def _out_proj(xs, mod, a, u, hf, hb, co, dd, w_out, pool_bd, pool_scale, ml_g, gm, g, b, keep_ctx):
    B, S, _ = xs.shape
    gw = GROUP_W
    hb_per_tile = TM // POOL_HALO
    n_halo = S // POOL_HALO
    t0 = 0 if keep_ctx else 1

    def tok(width):
        return pl.BlockSpec((1, TM, width), lambda b_, j: (b_, j + t0, 0))

    def tok_t(rows):
        return pl.BlockSpec((1, rows, TM), lambda b_, j: (b_, 0, j + t0))

    prev_spec = pl.BlockSpec((1, POOL_HALO, gw),
                             lambda b_, j: (b_, jnp.maximum((j + t0) * hb_per_tile - 1, 0), 0))
    next_spec = pl.BlockSpec((1, POOL_HALO, gw),
                             lambda b_, j: (b_, jnp.minimum((j + t0 + 1) * hb_per_tile, n_halo - 1), 0))
    return pl.pallas_call(
        functools.partial(_out_kernel, first_tile=t0),
        grid=(B, S // TM - t0),
        in_specs=[tok(D_MODEL), _mod_spec(keep_ctx), tok(gw), tok(gw), prev_spec, next_spec,
                  tok_t(gw), tok_t(gw), tok(gw), tok(gw),
                  _const_spec((D_MODEL, D_MODEL)), _const_spec((gw, gw)), _const_spec((1, gw)),
                  _const_spec((1, gw)), _const_spec((gw, gw)),
                  _const_spec((1, D_MODEL)), _const_spec((1, D_MODEL))],
        out_specs=pl.BlockSpec((1, TM, D_MODEL), lambda b_, j: (b_, j, 0)),
        out_shape=jax.ShapeDtypeStruct((B, S - t0 * TM, D_MODEL), F32),
        scratch_shapes=[pltpu.VMEM((TM + 2 * POOL_HALO, gw), F32)],
        compiler_params=_cparams(("parallel", "parallel")),
        name="mix_out_proj",
    )(xs, mod, a, u, u, u, hf, hb, co, dd, w_out, pool_bd, pool_scale, ml_g, gm, g, b)


def _rope_tables(L, Lc, dim):
    rows = L // GRID_W
    row = jnp.repeat(jnp.arange(rows), GRID_W).astype(F32)
    col = jnp.tile(jnp.arange(GRID_W), rows).astype(F32)
    axis_dim = dim // 2
    inv = ROPE_THETA ** (-jnp.arange(0, axis_dim, 2, dtype=F32) / axis_dim)
    ang = jnp.concatenate([row[:, None] * inv, col[:, None] * inv], axis=-1)
    cos, sin = jnp.cos(ang), jnp.sin(ang)
    reps = LANES // dim
    cos_p = jnp.tile(jnp.concatenate([cos, cos], axis=-1), (1, reps))
    sin_p = jnp.tile(jnp.concatenate([-sin, sin], axis=-1), (1, reps))
    cos_p = jnp.concatenate([jnp.ones((Lc, LANES), F32), cos_p], axis=0)
    sin_p = jnp.concatenate([jnp.zeros((Lc, LANES), F32), sin_p], axis=0)
    return cos_p, sin_p


def _reorder_w_in(w):
    g0 = 8 * GROUP_W
    g1 = g0 + 4 * ML_HEADS
    pad = jnp.zeros((w.shape[0], LANES - 4 * ML_HEADS), w.dtype)
    return jnp.concatenate([w[:, :g0], w[:, g1:], w[:, g0:g1], pad], axis=1).astype(BF16)


def _block_diag(blocks):
    n = blocks.shape[0]
    rows = []
    for i in range(n):
        rows.append(jnp.concatenate([blocks[i] if k == i else jnp.zeros_like(blocks[i])
                                     for k in range(n)], axis=1))
    return jnp.concatenate(rows, axis=0)


def kernel(x, c, ctx, c_ctx, w_ada, b_ada, ln_g, ln_b, ffn1_wi, ffn1_wo, ffn2_wi, ffn2_wo, w_in, w_out,
           diff_lambda, diff_norm_g, pool_w, pool_scale, ml_gate_b, ml_norm_g, gqa_qnorm_g, gqa_knorm_g):
    B, L, _ = x.shape
    Lc = ctx.shape[1]
    assert Lc == TM and L % KV_BLOCK == 0 and L % GRID_W == 0 and B + 1 <= MOD_ROWS
    depth = w_ada.shape[0]

    cc = jnp.concatenate([c_ctx[None], c, jnp.zeros((MOD_ROWS - 1 - B, D_MODEL), F32)], axis=0)
    mod_all = _modulation(cc, w_ada, b_ada).reshape(depth, MOD_ROWS, N_MOD, D_MODEL)

    rope = _rope_tables(L, Lc, DA_QK) + _rope_tables(L, Lc, GQA_DIM)
    gm = _block_diag(jnp.full((GROUP_W // HEAD_W, HEAD_W, HEAD_W), 1.0 / HEAD_W, BF16))

    xs = (ctx, x)
    for l in range(depth):
        last = l == depth - 1
        mod = mod_all[l]
        lam_init = 0.8 - 0.6 * math.exp(-0.3 * l)
        xs = _ffn(xs, mod, ffn1_wi[l].astype(BF16), ffn1_wo[l].astype(BF16), ln_g[l, 0], ln_b[l, 0], 0)
        gate_b = jnp.concatenate([ml_gate_b[l].reshape(1, -1),
                                  jnp.zeros((1, LANES - 4 * ML_HEADS), F32)], axis=1)
        qg = jnp.tile(gqa_qnorm_g[l], GQA_HEADS).reshape(1, -1)
        kg = jnp.tile(gqa_knorm_g[l], GQA_KV_HEADS).reshape(1, -1)
        (qat, ka, vat, u, cqt, ck, cvt, co, gates, gates_t, qdt, kd, vdt) = _in_proj(
            xs, mod, _reorder_w_in(w_in[l]), rope, qg, kg, gate_b, gm)
        a, dd = _attention(qat, qdt, ka, kd, vat, vdt, diff_lambda[l], diff_norm_g[l].reshape(1, -1),
                           lam_init)
        hf, hb = _mlstm(ck, cqt, cvt, gates, gates_t)
        xs = _out_proj(xs, mod, a, u, hf, hb, co, dd, w_out[l].astype(BF16),
                       _block_diag(pool_w[l]).astype(BF16), pool_scale[l].reshape(1, -1),
                       ml_norm_g[l].reshape(1, -1), gm, ln_g[l, 1].reshape(1, -1),
                       ln_b[l, 1].reshape(1, -1), keep_ctx=not last)
        xs = _ffn(xs, mod, ffn2_wi[l].astype(BF16), ffn2_wo[l].astype(BF16), ln_g[l, 2], ln_b[l, 2], 2,
                  has_ctx=not last)
    return xs
```
